```python
import jax, jax.numpy as jnp
from jax import lax
import numpy as np

D_MODEL = 1024
BATCH = 4
SEQ = 4096
DEPTH = 4
DEC_BATCH = 128
DEC_SEQ = 4
PAST_LEN = 2048
PAGE_SIZE = 128

N_HEADS = 8
HEAD_DIM = 64
ATTN_W = N_HEADS * HEAD_DIM
CONV_W = 512
CONV_K = 3
GMLP_W = 512
GMLP_GROUPS = 8
GMLP_GDIM = GMLP_W // GMLP_GROUPS
CHUNK = 128
N_BRANCH = 3
BRANCH_W = 512
MOBA_BLOCK = 256
MOBA_TOPK = 3
QUERY_BLOCK = 32
D_FF = 4 * D_MODEL
EPS = 1e-6
NEG = -1e30
IN_COLS = 3 * ATTN_W + 3 * CONV_W + 2 * GMLP_W + N_BRANCH * D_MODEL

kernel_name = 'hybrid_conv_gmlp_moba_decode_step'


def rmsnorm(x, g):
    xf = x.astype(jnp.float32)
    y = xf * lax.rsqrt(jnp.mean(xf * xf, axis=-1, keepdims=True) + EPS)
    return (y * g.astype(jnp.float32)).astype(x.dtype)


def split_proj(p):
    sizes = (ATTN_W, ATTN_W, ATTN_W, CONV_W, CONV_W, CONV_W, GMLP_W, GMLP_W, N_BRANCH * D_MODEL)
    cuts = [int(c) for c in np.cumsum(sizes)[:-1]]
    return jnp.split(p, cuts, axis=-1)


def mixer_inputs(x, norm_g, w_in, q_norm_g, k_norm_g):
    b, t = x.shape[:2]
    xn = rmsnorm(x, norm_g)
    q, k, v, cb, cc, ch, gu, gv, gl = split_proj(jnp.einsum('btd,dc->btc', xn, w_in))
    q = rmsnorm(q.reshape(b, t, N_HEADS, HEAD_DIM), q_norm_g)
    k = rmsnorm(k.reshape(b, t, N_HEADS, HEAD_DIM), k_norm_g)
    v = v.reshape(b, t, N_HEADS, HEAD_DIM)
    return q, k, v, cb, cc, ch, gu, gv, gl


def conv_mix(cb, cc, ch, prev, w):
    inp = cc * ch
    t = inp.shape[1]
    xp = jnp.concatenate([prev.astype(inp.dtype), inp], axis=1)
    y = xp[:, 0:t] * w[0]
    for i in range(1, CONV_K):
        y = y + xp[:, i:i + t] * w[i]
    return cb * y, xp[:, t:]


def gmlp_mix(u, v, norm_g, ws, bias):
    b, c, t, _ = v.shape
    vn = rmsnorm(v, norm_g).reshape(b, c, t, GMLP_GROUPS, GMLP_GDIM)
    causal = jnp.tril(jnp.ones((t, t), dtype=bool))
    w = jnp.where(causal, ws[:, :t, :t], jnp.zeros((), ws.dtype)).astype(v.dtype)
    mixed = jnp.einsum('gts,bcsgd->bctgd', w, vn) + jnp.transpose(bias[:, :t]).astype(v.dtype)[None, None, :, :, None]
    return u * mixed.reshape(b, c, t, GMLP_W)


def to_blocks(x):
    length = x.shape[2]
    lp = -(-length // MOBA_BLOCK) * MOBA_BLOCK
    xp = jnp.pad(x, ((0, 0), (0, 0), (0, lp - length), (0, 0)))
    return xp, xp.reshape(x.shape[0], x.shape[1], lp // MOBA_BLOCK, MOBA_BLOCK, x.shape[3])


def block_means(kb):
    return jnp.mean(kb.astype(jnp.float32), axis=3).astype(kb.dtype)


def moba_attend(qc, pos, kb, vb, km, ko, vo, own_pos, k_top):
    scale = HEAD_DIM ** -0.5
    s_own = jnp.einsum('bhqd,bhkd->bhqk', qc, ko).astype(jnp.float32) * scale
    s_own = jnp.where(own_pos[None, :] <= pos[:, None], s_own, NEG)
    if k_top == 0:
        p = jax.nn.softmax(s_own, axis=-1).astype(vo.dtype)
        return jnp.einsum('bhqk,bhkd->bhqd', p, vo)
    bsz, nh = qc.shape[:2]
    j = pos // MOBA_BLOCK
    gate = jnp.einsum('bhqd,bhnd->bhqn', qc, km).astype(jnp.float32)
    eligible = jnp.arange(km.shape[2], dtype=jnp.int32)[None, :] < j[:, None]
    gate = jnp.where(eligible, gate, NEG)
    _, idx = lax.top_k(gate, k_top)
    valid = idx < j[None, None, :, None]
    bi = jnp.arange(bsz)[:, None, None, None]
    hi = jnp.arange(nh)[None, :, None, None]
    ks = kb[bi, hi, idx]
    vs = vb[bi, hi, idx]
    s_sel = jnp.einsum('bhqd,bhqnkd->bhqnk', qc, ks).astype(jnp.float32) * scale
    s_sel = jnp.where(valid[..., None], s_sel, NEG).reshape(bsz, nh, qc.shape[2], k_top * MOBA_BLOCK)
    p = jax.nn.softmax(jnp.concatenate([s_sel, s_own], axis=-1), axis=-1).astype(vo.dtype)
    p_sel = p[..., :k_top * MOBA_BLOCK].reshape(bsz, nh, qc.shape[2], k_top, MOBA_BLOCK)
    p_own = p[..., k_top * MOBA_BLOCK:]
    return jnp.einsum('bhqnk,bhqnkd->bhqd', p_sel, vs) + jnp.einsum('bhqk,bhkd->bhqd', p_own, vo)


def moba_prompt(q, k, v):
    bsz, s = q.shape[:2]
    qt, kt, vt = (a.transpose(0, 2, 1, 3) for a in (q, k, v))
    kp, kb = to_blocks(kt)
    vp, vb = to_blocks(vt)
    km = block_means(kb)
    k_top = min(MOBA_TOPK, kb.shape[2] - 1)

    def one_block(c):
        start = c * QUERY_BLOCK
        qc = lax.dynamic_slice_in_dim(qt, start, QUERY_BLOCK, axis=2)
        pos = start + jnp.arange(QUERY_BLOCK, dtype=jnp.int32)
        own0 = (start // MOBA_BLOCK) * MOBA_BLOCK
        ko = lax.dynamic_slice_in_dim(kp, own0, MOBA_BLOCK, axis=2)
        vo = lax.dynamic_slice_in_dim(vp, own0, MOBA_BLOCK, axis=2)
        own_pos = own0 + jnp.arange(MOBA_BLOCK, dtype=jnp.int32)
        return moba_attend(qc, pos, kb, vb, km, ko, vo, own_pos, k_top)

    out = lax.map(one_block, jnp.arange(s // QUERY_BLOCK, dtype=jnp.int32))
    return out.transpose(1, 0, 3, 2, 4).reshape(bsz, s, ATTN_W)


def moba_sample(q, k_all, v_all):
    bsz, t = q.shape[:2]
    qt = q.transpose(0, 2, 1, 3)
    kp, kb = to_blocks(k_all.transpose(0, 2, 1, 3))
    vp, vb = to_blocks(v_all.transpose(0, 2, 1, 3))
    jo = PAST_LEN // MOBA_BLOCK
    k_top = min(MOBA_TOPK, jo)
    n_cand = max(jo, 1)
    kb_c, vb_c = kb[:, :, :n_cand], vb[:, :, :n_cand]
    km = block_means(kb_c)
    ko = kp[:, :, jo * MOBA_BLOCK:(jo + 1) * MOBA_BLOCK]
    vo = vp[:, :, jo * MOBA_BLOCK:(jo + 1) * MOBA_BLOCK]
    own_pos = jo * MOBA_BLOCK + jnp.arange(MOBA_BLOCK, dtype=jnp.int32)

    def one_token(i):
        qc = lax.dynamic_slice_in_dim(qt, i, 1, axis=2)
        pos = jnp.reshape(PAST_LEN + i, (1,))
        return moba_attend(qc, pos, kb_c, vb_c, km, ko, vo, own_pos, k_top)

    out = lax.map(one_token, jnp.arange(t, dtype=jnp.int32))
    return out[:, :, :, 0].transpose(1, 0, 2, 3).reshape(bsz, t, ATTN_W)


def merge_branches(gate_lin, gate_b, branches, w_branch, w_out):
    lead = gate_lin.shape[:-1]
    g = jax.nn.sigmoid((gate_lin.reshape(*lead, N_BRANCH, D_MODEL) + gate_b).astype(jnp.float32)).astype(gate_lin.dtype)
    br = jnp.stack(branches, axis=-2)
    proj = jnp.einsum('...ic,icd->...id', br, w_branch)
    return jnp.einsum('...d,de->...e', jnp.sum(g * proj, axis=-2), w_out)


def channel_mlp(x, norm_g, w_up, w_down):
    h = jax.nn.relu(jnp.einsum('btd,df->btf', rmsnorm(x, norm_g), w_up))
    return jnp.einsum('btf,fd->btd', jnp.square(h), w_down)


def setup_inputs(seed: int = 0) -> dict:
    key = jax.random.key(seed)
    ks = jax.random.split(key, 24)
    n_pages = PAST_LEN // PAGE_SIZE
    n_used = DEC_BATCH * n_pages
    n_phys = n_used + n_used // 4
    f32 = jnp.float32
    nrm = lambda k, shape, s: jax.random.normal(k, shape, f32) * s
    page_table = jax.random.permutation(ks[5], n_phys)[:n_used].reshape(DEC_BATCH, n_pages).astype(jnp.int32)
    return {
        'x_prompt': nrm(ks[0], (BATCH, SEQ, D_MODEL), 1.0),
        'x_sample': nrm(ks[1], (DEC_BATCH, DEC_SEQ, D_MODEL), 1.0),
        'cache_k': nrm(ks[2], (DEPTH, n_phys, PAGE_SIZE, N_HEADS, HEAD_DIM), 1.0),
        'cache_v': nrm(ks[3], (DEPTH, n_phys, PAGE_SIZE, N_HEADS, HEAD_DIM), 1.0),
        'state_conv': nrm(ks[4], (DEPTH, DEC_BATCH, CONV_K - 1, CONV_W), 1.0),
        'page_table': page_table,
        'norm1_g': 1.0 + nrm(ks[6], (DEPTH, D_MODEL), 0.05),
        'w_in': nrm(ks[7], (DEPTH, D_MODEL, IN_COLS), D_MODEL ** -0.5),
        'q_norm_g': 1.0 + nrm(ks[8], (DEPTH, HEAD_DIM), 0.05),
        'k_norm_g': 1.0 + nrm(ks[9], (DEPTH, HEAD_DIM), 0.05),
        'conv_w': nrm(ks[10], (DEPTH, CONV_K, CONV_W), CONV_K ** -0.5),
        'gmlp_norm_g': 1.0 + nrm(ks[11], (DEPTH, GMLP_W), 0.05),
        'gmlp_ws': nrm(ks[12], (DEPTH, GMLP_GROUPS, CHUNK, CHUNK), CHUNK ** -0.5),
        'gmlp_b': 1.0 + nrm(ks[13], (DEPTH, GMLP_GROUPS, CHUNK), 0.1),
        'gate_b': nrm(ks[14], (DEPTH, N_BRANCH, D_MODEL), 0.01),
        'w_branch': nrm(ks[15], (DEPTH, N_BRANCH, BRANCH_W, D_MODEL), BRANCH_W ** -0.5),
        'w_out': nrm(ks[16], (DEPTH, D_MODEL, D_MODEL), D_MODEL ** -0.5),
        'norm2_g': 1.0 + nrm(ks[17], (DEPTH, D_MODEL), 0.05),
        'w_up': nrm(ks[18], (DEPTH, D_MODEL, D_FF), D_MODEL ** -0.5),
        'w_down': nrm(ks[19], (DEPTH, D_FF, D_MODEL), 0.5 * D_FF ** -0.5),
    }


def reference(x_prompt, x_sample, cache_k, cache_v, state_conv, page_table, norm1_g, w_in, q_norm_g, k_norm_g, conv_w, gmlp_norm_g, gmlp_ws, gmlp_b, gate_b, w_branch, w_out, norm2_g, w_up, w_down):
    bsz, s, _ = x_prompt.shape
    dbsz, t, _ = x_sample.shape
    hp, hs = x_prompt, x_sample
    kp_l, vp_l, ks_l, vs_l, cp_l, cs_l, gv_l = [], [], [], [], [], [], []
    for l in range(DEPTH):
        q, k, v, cb, cc, ch, gu, gv, gl = mixer_inputs(hp, norm1_g[l], w_in[l], q_norm_g[l], k_norm_g[l])
        attn = moba_prompt(q, k, v)
        conv, conv_st = conv_mix(cb, cc, ch, jnp.zeros((bsz, CONV_K - 1, CONV_W), hp.dtype), conv_w[l])
        gm = gmlp_mix(gu.reshape(bsz, s // CHUNK, CHUNK, GMLP_W), gv.reshape(bsz, s // CHUNK, CHUNK, GMLP_W),
                      gmlp_norm_g[l], gmlp_ws[l], gmlp_b[l]).reshape(bsz, s, GMLP_W)
        hp = hp + merge_branches(gl, gate_b[l], (attn, conv, gm), w_branch[l], w_out[l])
        hp = hp + channel_mlp(hp, norm2_g[l], w_up[l], w_down[l])
        kp_l.append(k)
        vp_l.append(v)
        cp_l.append(conv_st)
        q, k, v, cb, cc, ch, gu, gv, gl = mixer_inputs(hs, norm1_g[l], w_in[l], q_norm_g[l], k_norm_g[l])
        k_past = cache_k[l, page_table].reshape(dbsz, -1, N_HEADS, HEAD_DIM).astype(k.dtype)
        v_past = cache_v[l, page_table].reshape(dbsz, -1, N_HEADS, HEAD_DIM).astype(v.dtype)
        attn = moba_sample(q, jnp.concatenate([k_past, k], axis=1), jnp.concatenate([v_past, v], axis=1))
        conv, conv_st = conv_mix(cb, cc, ch, state_conv[l], conv_w[l])
        gm = gmlp_mix(gu[:, None], gv[:, None], gmlp_norm_g[l], gmlp_ws[l], gmlp_b[l])[:, 0]
        hs = hs + merge_branches(gl, gate_b[l], (attn, conv, gm), w_branch[l], w_out[l])
        hs = hs + channel_mlp(hs, norm2_g[l], w_up[l], w_down[l])
        ks_l.append(k)
        vs_l.append(v)
        cs_l.append(conv_st)
        gv_l.append(gv)
    return (hp, hs, jnp.stack(kp_l), jnp.stack(vp_l), jnp.stack(ks_l), jnp.stack(vs_l), jnp.stack(cp_l), jnp.stack(cs_l), jnp.stack(gv_l))
```

```python
import functools

import numpy as np
import jax
import jax.numpy as jnp
from jax import lax
from jax.experimental import pallas as pl
from jax.experimental.pallas import tpu as pltpu

D_MODEL = 1024
N_HEADS = 8
HEAD_DIM = 64
ATTN_W = N_HEADS * HEAD_DIM
CONV_W = 512
GMLP_W = 512
GMLP_GROUPS = 8
CHUNK = 128
N_BRANCH = 3
MOBA_BLOCK = 256
MOBA_TOPK = 3
PAGE_SIZE = 128
D_FF = 4 * D_MODEL
EPS = 1e-6
NEG = -1e30
IN_COLS = 3 * ATTN_W + 3 * CONV_W + 2 * GMLP_W + N_BRANCH * D_MODEL
COL_Q, COL_K, COL_V, COL_CB, COL_CC, COL_CH, COL_GU, COL_GV, COL_GL = 0, 1, 2, 3, 4, 5, 6, 7, 8

LANES = 128
V7X_VMEM_LIMIT = 56 * 1024 * 1024

BF16 = jnp.bfloat16
F32 = jnp.float32
NT_DIMS = (((1,), (1,)), ((), ()))


def _rms_scale(x):
    return lax.rsqrt(jnp.mean(x * x, axis=-1, keepdims=True) + EPS)


def _inproj_kernel(x_ref, g_ref, w_ref, qkg_ref, gmat_ref, o_ref, xn_ref):
    j = pl.program_id(1)

    @pl.when(j == 0)
    def _():
        x = x_ref[...]
        xn_ref[...] = (x * _rms_scale(x) * g_ref[...]).astype(BF16)

    acc = jnp.dot(xn_ref[...], w_ref[...], preferred_element_type=F32)

    @pl.when(j <= COL_K)
    def _():
        sq = acc * acc
        hi = sq.astype(BF16)
        lo = (sq - hi.astype(F32)).astype(BF16)
        gmat = gmat_ref[...]
        ms = jnp.dot(hi, gmat, preferred_element_type=F32) + jnp.dot(lo, gmat, preferred_element_type=F32)
        o_ref[...] = acc * lax.rsqrt(ms + EPS) * qkg_ref[...]

    @pl.when(j > COL_K)
    def _():
        o_ref[...] = acc


def _in_proj(x, g, w_bf, qkg, gmat, tm):
    m = x.shape[0]
    tn = 512
    return pl.pallas_call(
        _inproj_kernel,
        out_shape=jax.ShapeDtypeStruct((m, IN_COLS), F32),
        grid=(m // tm, IN_COLS // tn),
        in_specs=[
            pl.BlockSpec((tm, D_MODEL), lambda i, j: (i, 0)),
            pl.BlockSpec((1, D_MODEL), lambda i, j: (0, 0)),
            pl.BlockSpec((D_MODEL, tn), lambda i, j: (0, j)),
            pl.BlockSpec((None, 1, tn), lambda i, j: (jnp.minimum(j, COL_K), 0, 0)),
            pl.BlockSpec((tn, tn), lambda i, j: (0, 0)),
        ],
        out_specs=pl.BlockSpec((tm, tn), lambda i, j: (i, j)),
        scratch_shapes=[pltpu.VMEM((tm, D_MODEL), BF16)],
        compiler_params=pltpu.CompilerParams(
            dimension_semantics=("parallel", "arbitrary"), vmem_limit_bytes=V7X_VMEM_LIMIT),
        name="in_proj",
    )(x, g, w_bf, qkg, gmat)


def _attn_prompt_kernel(q_ref, k_ref, v_ref, o_ref, kt_ref, vt_ref, kmp_ref, qa_ref, m_ref, acc_ref):
    t = pl.program_id(2)
    seq = k_ref.shape[0]
    nblk = seq // MOBA_BLOCK
    lane = lax.broadcasted_iota(jnp.int32, (1, LANES), 1)
    lane_f = lane.astype(F32)
    lo_half = lane < HEAD_DIM

    @pl.when(t == 0)
    def _init():
        kf = k_ref[...]
        vb = v_ref[...].astype(BF16)
        rowblk = lax.shift_right_logical(lax.broadcasted_iota(jnp.int32, (seq, LANES), 0), 8)
        onehot = rowblk == lax.broadcasted_iota(jnp.int32, (seq, LANES), 1)
        kt_ref[:, 0:LANES] = kf.astype(BF16)
        kt_ref[:, LANES:2 * LANES] = jnp.where(onehot, 1.0, 0.0).astype(BF16)
        one = jnp.ones_like(vb)
        vt_ref[0] = jnp.where(lo_half, vb, one)
        vt_ref[1] = jnp.where(lo_half, one, vb)
        km = jnp.sum(kf.reshape(nblk, MOBA_BLOCK, LANES), axis=1) * (1.0 / MOBA_BLOCK)
        kmp_ref[...] = jnp.zeros((LANES, LANES), F32)
        kmp_ref[0:nblk, :] = km

    q = q_ref[...]
    kmp = kmp_ref[...]
    row = lax.broadcasted_iota(jnp.int32, (MOBA_BLOCK, MOBA_BLOCK), 0)
    col = lax.broadcasted_iota(jnp.int32, (MOBA_BLOCK, MOBA_BLOCK), 1)
    koff = pl.multiple_of(t * MOBA_BLOCK, MOBA_BLOCK)

    for h in range(2):
        qh = jnp.where(lo_half if h == 0 else jnp.logical_not(lo_half), q, 0.0)
        gate = lax.dot_general(qh, kmp, NT_DIMS, precision=lax.Precision.HIGHEST, preferred_element_type=F32)
        g = jnp.where(lane < t, gate, -jnp.inf)
        sel = jnp.where(lane == t, 1.0, 0.0) + jnp.zeros_like(g)
        for _ in range(MOBA_TOPK):
            mx = jnp.max(g, axis=1, keepdims=True)
            first = jnp.min(jnp.where(g == mx, lane_f, float(LANES)), axis=1, keepdims=True)
            first = jnp.where(mx > -jnp.inf, first, -1.0)
            pick = lane_f == first
            sel = jnp.where(pick, 1.0, sel)
            g = jnp.where(pick, -jnp.inf, g)
        bias = jnp.where(sel > 0.5, 0.0, NEG).astype(BF16)
        qa = jnp.concatenate([(qh * (HEAD_DIM ** -0.5)).astype(BF16), bias], axis=1)
        qa_ref[h] = qa

        s = lax.dot_general(qa, kt_ref[pl.ds(koff, MOBA_BLOCK), :], NT_DIMS, preferred_element_type=F32)
        s = jnp.where(col <= row, s, NEG)
        m0 = jnp.max(jnp.maximum(s[:, :LANES], s[:, LANES:]), axis=1, keepdims=True)
        m0 = jnp.broadcast_to(m0, (MOBA_BLOCK, LANES))
        p = jnp.concatenate([jnp.exp(s[:, :LANES] - m0), jnp.exp(s[:, LANES:] - m0)], axis=1).astype(BF16)
        m_ref[h] = m0
        acc_ref[h] = jnp.dot(p, vt_ref[h, pl.ds(koff, MOBA_BLOCK), :], preferred_element_type=F32)

    def body(n, carry):
        off = pl.multiple_of(n * MOBA_BLOCK, MOBA_BLOCK)
        kblk = kt_ref[pl.ds(off, MOBA_BLOCK), :]
        for h in range(2):
            s = lax.dot_general(qa_ref[h], kblk, NT_DIMS, preferred_element_type=F32)
            m_old = m_ref[h]
            mx = jnp.max(jnp.maximum(s[:, :LANES], s[:, LANES:]), axis=1, keepdims=True)
            m_new = jnp.maximum(m_old, mx)
            alpha = jnp.exp(m_old - m_new)
            p = jnp.concatenate([jnp.exp(s[:, :LANES] - m_new), jnp.exp(s[:, LANES:] - m_new)], axis=1).astype(BF16)
            m_ref[h] = m_new
            acc_ref[h] = alpha * acc_ref[h] + jnp.dot(p, vt_ref[h, pl.ds(off, MOBA_BLOCK), :],
                                                     preferred_element_type=F32)
        return carry

    lax.fori_loop(0, t, body, 0)

    a0 = acc_ref[0]
    a1 = acc_ref[1]
    o0 = a0 / pltpu.roll(a0, HEAD_DIM, axis=1)
    o1 = a1 / pltpu.roll(a1, HEAD_DIM, axis=1)
    o_ref[...] = jnp.where(lo_half, o0, o1).astype(o_ref.dtype)


def _attn_prompt(slab, batch, seq):
    m = slab.shape[0]
    nq = seq // MOBA_BLOCK
    npair = ATTN_W // LANES
    cpb = 512 // LANES
    return pl.pallas_call(
        _attn_prompt_kernel,
        out_shape=jax.ShapeDtypeStruct((m, ATTN_W), BF16),
        grid=(batch, npair, nq),
        in_specs=[
            pl.BlockSpec((MOBA_BLOCK, LANES), lambda b, p, t: (b * nq + t, COL_Q * cpb + p)),
            pl.BlockSpec((seq, LANES), lambda b, p, t: (b, COL_K * cpb + p)),
            pl.BlockSpec((seq, LANES), lambda b, p, t: (b, COL_V * cpb + p)),
        ],
        out_specs=pl.BlockSpec((MOBA_BLOCK, LANES), lambda b, p, t: (b * nq + t, p)),
        scratch_shapes=[
            pltpu.VMEM((seq, 2 * LANES), BF16),
            pltpu.VMEM((2, seq, LANES), BF16),
            pltpu.VMEM((LANES, LANES), F32),
            pltpu.VMEM((2, MOBA_BLOCK, 2 * LANES), BF16),
            pltpu.VMEM((2, MOBA_BLOCK, LANES), F32),
            pltpu.VMEM((2, MOBA_BLOCK, LANES), F32),
        ],
        compiler_params=pltpu.CompilerParams(
            dimension_semantics=("parallel", "parallel", "arbitrary"), vmem_limit_bytes=V7X_VMEM_LIMIT),
        name="attn_prompt",
    )(slab, slab, slab)


def _attn_sample_kernel(pt_ref, q_ref, kn_ref, vn_ref, *refs, n_pages):
    del pt_ref
    kp_refs = refs[:n_pages]
    vp_refs = refs[n_pages:2 * n_pages]
    o_ref = refs[2 * n_pages]
    t_new = q_ref.shape[0]
    rows = t_new * N_HEADS
    pages_per_blk = MOBA_BLOCK // PAGE_SIZE
    n_past = n_pages // pages_per_blk

    q = q_ref[...]
    kn = kn_ref[...]
    vn = vn_ref[...]
    head_of_lane = lax.shift_right_logical(lax.broadcasted_iota(jnp.int32, (N_HEADS, ATTN_W), 1), 6)
    hm = head_of_lane == lax.broadcasted_iota(jnp.int32, (N_HEADS, ATTN_W), 0)
    qf = jnp.concatenate(
        [jnp.where(hm, jnp.broadcast_to(q[t:t + 1, :], (N_HEADS, ATTN_W)), 0.0) for t in range(t_new)], axis=0)
    hm_rows = jnp.concatenate([hm] * t_new, axis=0)
    trow = lax.shift_right_logical(lax.broadcasted_iota(jnp.int32, (rows, 1), 0), 3)

    kp = [r[...] for r in kp_refs]
    gates = []
    for n in range(n_past):
        km = jnp.sum(kp[2 * n], axis=0, keepdims=True) + jnp.sum(kp[2 * n + 1], axis=0, keepdims=True)
        km = km * (1.0 / MOBA_BLOCK)
        gates.append(jnp.sum(qf * km, axis=1, keepdims=True))
    sel = []
    for n in range(n_past):
        rank = jnp.zeros((rows, 1), F32)
        for mm in range(n_past):
            if mm == n:
                continue
            beats = (gates[mm] >= gates[n]) if mm < n else (gates[mm] > gates[n])
            rank = rank + jnp.where(beats, 1.0, 0.0)
        sel.append(rank < float(min(MOBA_TOPK, n_past)))

    qs = qf * (HEAD_DIM ** -0.5)
    qb = qs.astype(BF16)
    s_past = []
    for p in range(n_pages):
        s = lax.dot_general(qb, kp[p].astype(BF16), NT_DIMS, preferred_element_type=F32)
        s_past.append(jnp.where(sel[p // pages_per_blk], s, NEG))
    s_own = []
    for i in range(t_new):
        s = jnp.sum(qs * kn[i:i + 1, :], axis=1, keepdims=True)
        s_own.append(jnp.where(trow >= i, s, NEG))

    m = s_own[0]
    for s in s_own[1:]:
        m = jnp.maximum(m, s)
    for s in s_past:
        m = jnp.maximum(m, jnp.max(s, axis=1, keepdims=True))

    acc = jnp.zeros((rows, ATTN_W), F32)
    den = jnp.zeros((rows, 1), F32)
    for p in range(n_pages):
        e = jnp.exp(s_past[p] - m)
        den = den + jnp.sum(e, axis=1, keepdims=True)
        acc = acc + jnp.dot(e.astype(BF16), vp_refs[p][...].astype(BF16), preferred_element_type=F32)
    for i in range(t_new):
        e = jnp.exp(s_own[i] - m)
        den = den + e
        acc = acc + e * vn[i:i + 1, :]
    o = jnp.where(hm_rows, acc / den, 0.0)
    o_ref[...] = jnp.sum(o.reshape(t_new, N_HEADS, ATTN_W), axis=1)


def _attn_sample(layer, pt_flat, q3, k3, v3, cache_k4, cache_v4, n_pages):
    nseq, t_new, _ = q3.shape

    def page_spec(p):
        return pl.BlockSpec((None, None, PAGE_SIZE, ATTN_W),
                            lambda s, pt: (layer, pt[s * n_pages + p], 0, 0))

    new_spec = pl.BlockSpec((None, t_new, ATTN_W), lambda s, pt: (s, 0, 0))
    grid_spec = pltpu.PrefetchScalarGridSpec(
        num_scalar_prefetch=1,
        grid=(nseq,),
        in_specs=[new_spec, new_spec, new_spec]
        + [page_spec(p) for p in range(n_pages)] + [page_spec(p) for p in range(n_pages)],
        out_specs=pl.BlockSpec((None, t_new, ATTN_W), lambda s, pt: (s, 0, 0)),
    )
    return pl.pallas_call(
        functools.partial(_attn_sample_kernel, n_pages=n_pages),
        out_shape=jax.ShapeDtypeStruct((nseq, t_new, ATTN_W), F32),
        grid_spec=grid_spec,
        compiler_params=pltpu.CompilerParams(
            dimension_semantics=("arbitrary",), vmem_limit_bytes=V7X_VMEM_LIMIT),
        name="attn_sample",
    )(pt_flat, q3, k3, v3, *([cache_k4] * n_pages), *([cache_v4] * n_pages))


def _merge_tail(conv, gm, attn_ref, gl_refs, gb_ref, wb_ref, wo_ref, x_ref, y_ref):
    branches = (attn_ref[...].astype(BF16), conv.astype(BF16), gm.astype(BF16))
    merged = None
    for i in range(N_BRANCH):
        gate = jax.nn.sigmoid(gl_refs[i][...] + gb_ref[i:i + 1, :])
        term = gate * jnp.dot(branches[i], wb_ref[i], preferred_element_type=F32)
        merged = term if merged is None else merged + term
    y_ref[...] = x_ref[...] + jnp.dot(merged.astype(BF16), wo_ref[...], preferred_element_type=F32)


def _conv_taps(inp, m1, m2, cw_ref):
    return m2 * cw_ref[0:1, :] + m1 * cw_ref[1:2, :] + inp * cw_ref[2:3, :]


def _merge_prompt_kernel(cb_ref, cc_ref, ch_ref, gu_ref, gv_ref, gl0_ref, gl1_ref, gl2_ref, ccp_ref, chp_ref,
                         attn_ref, x_ref, cw_ref, gn_ref, ws_ref, gbias_ref, gb_ref, wb_ref, wo_ref,
                         y_ref, tail_ref, *, tiles_per_seq):
    i = pl.program_id(0)
    tm = cb_ref.shape[0]
    inp = cc_ref[...] * ch_ref[...]
    prev = ccp_ref[...] * chp_ref[...]
    prev = jnp.where(i % tiles_per_seq == 0, 0.0, prev)
    p0 = prev[6:7, :]
    p1 = prev[7:8, :]
    rowi = lax.broadcasted_iota(jnp.int32, (tm, 1), 0)
    m1 = jnp.where(rowi == 0, p1, pltpu.roll(inp, 1, axis=0))
    m2 = jnp.where(rowi == 0, p0, jnp.where(rowi == 1, p1, pltpu.roll(inp, 2, axis=0)))
    conv = cb_ref[...] * _conv_taps(inp, m1, m2, cw_ref)
    tail_ref[...] = inp[tm - 8:tm, :]

    gv = gv_ref[...]
    vn = gv * _rms_scale(gv) * gn_ref[...]
    lane = lax.broadcasted_iota(jnp.int32, (1, LANES), 1)
    lo_half = lane < (GMLP_W // GMLP_GROUPS)
    wrow = lax.broadcasted_iota(jnp.int32, (CHUNK, 2 * CHUNK), 0)
    wcol = lax.broadcasted_iota(jnp.int32, (CHUNK, 2 * CHUNK), 1) & (CHUNK - 1)
    cols = []
    for c in range(GMLP_W // LANES):
        w = jnp.where(wcol <= wrow, ws_ref[c], 0.0).astype(BF16)
        vc = vn[:, c * LANES:(c + 1) * LANES]
        chunks = []
        for k in range(tm // CHUNK):
            vck = vc[k * CHUNK:(k + 1) * CHUNK, :]
            rhs = jnp.concatenate([jnp.where(lo_half, vck, 0.0), jnp.where(lo_half, 0.0, vck)], axis=0)
            mixed = jnp.dot(w, rhs.astype(BF16), preferred_element_type=F32)
            chunks.append(mixed + gbias_ref[:, c * LANES:(c + 1) * LANES])
        cols.append(jnp.concatenate(chunks, axis=0))
    gm = gu_ref[...] * jnp.concatenate(cols, axis=1)

    _merge_tail(conv, gm, attn_ref, (gl0_ref, gl1_ref, gl2_ref), gb_ref, wb_ref, wo_ref, x_ref, y_ref)


def _merge_sample_kernel(cb_ref, cc_ref, ch_ref, gu_ref, gv_ref, gl0_ref, gl1_ref, gl2_ref, prev1_ref, prev2_ref,
                         attn_ref, x_ref, cw_ref, gn_ref, coef_ref, gbias_ref, gb_ref, wb_ref, wo_ref,
                         y_ref, inp_ref, *, t_new):
    tm = cb_ref.shape[0]
    inp = cc_ref[...] * ch_ref[...]
    tpos = lax.broadcasted_iota(jnp.int32, (tm, 1), 0) & (t_new - 1)
    m1 = jnp.where(tpos == 0, prev1_ref[...], pltpu.roll(inp, 1, axis=0))
    m2 = jnp.where(tpos < 2, prev2_ref[...], pltpu.roll(inp, 2, axis=0))
    conv = cb_ref[...] * _conv_taps(inp, m1, m2, cw_ref)
    inp_ref[...] = inp

    gv = gv_ref[...]
    vn = gv * _rms_scale(gv) * gn_ref[...]
    mixed = coef_ref[0] * vn + gbias_ref[...]
    for d in range(1, t_new):
        mixed = mixed + coef_ref[d] * pltpu.roll(vn, d, axis=0)
    gm = gu_ref[...] * mixed

    _merge_tail(conv, gm, attn_ref, (gl0_ref, gl1_ref, gl2_ref), gb_ref, wb_ref, wo_ref, x_ref, y_ref)


def _slab_specs(tm):
    col = lambda c: pl.BlockSpec((tm, 512), lambda i, c=c: (i, c))
    gl = lambda g: pl.BlockSpec((tm, D_MODEL), lambda i, g=g: (i, COL_GL // 2 + g))
    return [col(COL_CB), col(COL_CC), col(COL_CH), col(COL_GU), col(COL_GV), gl(0), gl(1), gl(2)]


def _full(shape):
    return pl.BlockSpec(shape, lambda i: (0,) * len(shape))


def _merge_prompt(slab, attn, x, cw, gn, ws_cat, gbias, gb, wb, wo, seq, tm):
    m = slab.shape[0]
    ntiles = m // tm
    halo = lambda c: pl.BlockSpec((8, 512), lambda i, c=c: (jnp.maximum(i * (tm // 8) - 1, 0), c))
    y, tail = pl.pallas_call(
        functools.partial(_merge_prompt_kernel, tiles_per_seq=seq // tm),
        out_shape=(jax.ShapeDtypeStruct((m, D_MODEL), F32), jax.ShapeDtypeStruct((ntiles, 8, CONV_W), F32)),
        grid=(ntiles,),
        in_specs=_slab_specs(tm) + [halo(COL_CC), halo(COL_CH),
                                    pl.BlockSpec((tm, ATTN_W), lambda i: (i, 0)),
                                    pl.BlockSpec((tm, D_MODEL), lambda i: (i, 0)),
                                    _full(cw.shape), _full(gn.shape), _full(ws_cat.shape), _full(gbias.shape),
                                    _full(gb.shape), _full(wb.shape), _full(wo.shape)],
        out_specs=(pl.BlockSpec((tm, D_MODEL), lambda i: (i, 0)),
                   pl.BlockSpec((None, 8, CONV_W), lambda i: (i, 0, 0))),
        compiler_params=pltpu.CompilerParams(
            dimension_semantics=("parallel",), vmem_limit_bytes=V7X_VMEM_LIMIT),
        name="merge_prompt",
    )(*([slab] * 10), attn, x, cw, gn, ws_cat, gbias, gb, wb, wo)
    return y, tail


def _merge_sample(slab, prev1, prev2, attn, x, cw, gn, coef, gbias_rows, gb, wb, wo, t_new, tm):
    m = slab.shape[0]
    rows = lambda w: pl.BlockSpec((tm, w), lambda i: (i, 0))
    y, inp = pl.pallas_call(
        functools.partial(_merge_sample_kernel, t_new=t_new),
        out_shape=(jax.ShapeDtypeStruct((m, D_MODEL), F32), jax.ShapeDtypeStruct((m, CONV_W), F32)),
        grid=(m // tm,),
        in_specs=_slab_specs(tm) + [rows(CONV_W), rows(CONV_W), rows(ATTN_W), rows(D_MODEL),
                                    _full(cw.shape), _full(gn.shape),
                                    pl.BlockSpec((t_new, tm, GMLP_W), lambda i: (0, i, 0)), rows(GMLP_W),
                                    _full(gb.shape), _full(wb.shape), _full(wo.shape)],
        out_specs=(rows(D_MODEL), rows(CONV_W)),
        compiler_params=pltpu.CompilerParams(
            dimension_semantics=("parallel",), vmem_limit_bytes=V7X_VMEM_LIMIT),
        name="merge_sample",
    )(*([slab] * 8), prev1, prev2, attn, x, cw, gn, coef, gbias_rows, gb, wb, wo)
    return y, inp


def _mlp_kernel(x_ref, g_ref, wu_ref, wd_ref, o_ref, xn_ref, acc_ref):
    f = pl.program_id(1)

    @pl.when(f == 0)
    def _():
        x = x_ref[...]
        xn_ref[...] = (x * _rms_scale(x) * g_ref[...]).astype(BF16)
        acc_ref[...] = jnp.zeros_like(acc_ref)

    h = jnp.maximum(jnp.dot(xn_ref[...], wu_ref[...], preferred_element_type=F32), 0.0)
    acc_ref[...] += jnp.dot((h * h).astype(BF16), wd_ref[...], preferred_element_type=F32)

    @pl.when(f == pl.num_programs(1) - 1)
    def _():
        o_ref[...] = x_ref[...] + acc_ref[...]


def _mlp(x, g, wu, wd, tm):
    m = x.shape[0]
    tf = 1024
    return pl.pallas_call(
        _mlp_kernel,
        out_shape=jax.ShapeDtypeStruct((m, D_MODEL), F32),
        grid=(m // tm, D_FF // tf),
        in_specs=[
            pl.BlockSpec((tm, D_MODEL), lambda i, f: (i, 0)),
            pl.BlockSpec((1, D_MODEL), lambda i, f: (0, 0)),
            pl.BlockSpec((D_MODEL, tf), lambda i, f: (0, f)),
            pl.BlockSpec((tf, D_MODEL), lambda i, f: (f, 0)),
        ],
        out_specs=pl.BlockSpec((tm, D_MODEL), lambda i, f: (i, 0)),
        scratch_shapes=[pltpu.VMEM((tm, D_MODEL), BF16), pltpu.VMEM((tm, D_MODEL), F32)],
        compiler_params=pltpu.CompilerParams(
            dimension_semantics=("parallel", "arbitrary"), vmem_limit_bytes=V7X_VMEM_LIMIT),
        name="mlp",
    )(x, g, wu, wd)


def _head_mean_matrix():
    idx = np.arange(512) // HEAD_DIM
    return jnp.asarray((idx[:, None] == idx[None, :]).astype(np.float32) / HEAD_DIM, dtype=BF16)


def kernel(x_prompt, x_sample, cache_k, cache_v, state_conv, page_table, norm1_g, w_in, q_norm_g, k_norm_g,
           conv_w, gmlp_norm_g, gmlp_ws, gmlp_b, gate_b, w_branch, w_out, norm2_g, w_up, w_down):
    batch, seq, _ = x_prompt.shape
    nseq, t_new, _ = x_sample.shape
    depth = w_in.shape[0]
    n_phys = cache_k.shape[1]
    n_pages = page_table.shape[1]
    gdim = GMLP_W // GMLP_GROUPS
    assert t_new == 4 and seq % MOBA_BLOCK == 0 and (n_pages * PAGE_SIZE) % MOBA_BLOCK == 0

    hp = x_prompt.reshape(batch * seq, D_MODEL)
    hs = x_sample.reshape(nseq * t_new, D_MODEL)
    ms = hs.shape[0]
    cache_k4 = cache_k.reshape(depth, n_phys, PAGE_SIZE, ATTN_W)
    cache_v4 = cache_v.reshape(depth, n_phys, PAGE_SIZE, ATTN_W)
    pt_flat = page_table.reshape(-1).astype(jnp.int32)
    gmat = _head_mean_matrix()

    kp_l, vp_l, ks_l, vs_l, cp_l, cs_l, gv_l = [], [], [], [], [], [], []
    for l in range(depth):
        w_in_b = w_in[l].astype(BF16)
        wb_b = w_branch[l].astype(BF16)
        wo_b = w_out[l].astype(BF16)
        wu_b = w_up[l].astype(BF16)
        wd_b = w_down[l].astype(BF16)
        g1 = norm1_g[l].reshape(1, D_MODEL)
        g2 = norm2_g[l].reshape(1, D_MODEL)
        qkg = jnp.stack([jnp.tile(q_norm_g[l], N_HEADS), jnp.tile(k_norm_g[l], N_HEADS)]).reshape(2, 1, ATTN_W)
        gn = gmlp_norm_g[l].reshape(1, GMLP_W)
        ws = gmlp_ws[l]
        ws_cat = ws.reshape(GMLP_GROUPS // 2, 2, CHUNK, CHUNK).transpose(0, 2, 1, 3).reshape(
            GMLP_GROUPS // 2, CHUNK, 2 * CHUNK)
        gbias = jnp.repeat(gmlp_b[l].T, gdim, axis=1)

        slab = _in_proj(hp, g1, w_in_b, qkg, gmat, tm=2048)
        attn = _attn_prompt(slab, batch, seq)
        tm_p = 256
        hp, tail = _merge_prompt(slab, attn, hp, conv_w[l], gn, ws_cat, gbias, gate_b[l], wb_b, wo_b, seq, tm_p)
        hp = _mlp(hp, g2, wu_b, wd_b, tm=1024)
        kp_l.append(slab[:, COL_K * 512:(COL_K + 1) * 512].reshape(batch, seq, N_HEADS, HEAD_DIM))
        vp_l.append(slab[:, COL_V * 512:(COL_V + 1) * 512].reshape(batch, seq, N_HEADS, HEAD_DIM))
        cp_l.append(tail.reshape(batch, seq // tm_p, 8, CONV_W)[:, -1, 6:8, :])

        slab_s = _in_proj(hs, g1, w_in_b, qkg, gmat, tm=ms)
        q3 = slab_s[:, COL_Q * 512:(COL_Q + 1) * 512].reshape(nseq, t_new, ATTN_W)
        k3 = slab_s[:, COL_K * 512:(COL_K + 1) * 512].reshape(nseq, t_new, ATTN_W)
        v3 = slab_s[:, COL_V * 512:(COL_V + 1) * 512].reshape(nseq, t_new, ATTN_W)
        attn_s = _attn_sample(l, pt_flat, q3, k3, v3, cache_k4, cache_v4, n_pages).reshape(ms, ATTN_W)
        st = state_conv[l]
        zero = jnp.zeros((nseq, 1, CONV_W), F32)
        prev1 = jnp.concatenate([st[:, 1:2], zero, zero, zero], axis=1).reshape(ms, CONV_W)
        prev2 = jnp.concatenate([st[:, 0:1], st[:, 1:2], zero, zero], axis=1).reshape(ms, CONV_W)
        tt = np.arange(t_new)
        coef_rows = []
        for d in range(t_new):
            wd_sel = ws[:, tt, np.maximum(tt - d, 0)] * jnp.asarray((tt >= d).astype(np.float32))
            coef_rows.append(jnp.repeat(wd_sel.T, gdim, axis=1))
        coef = jnp.tile(jnp.stack(coef_rows), (1, nseq, 1))
        gbias_rows = jnp.tile(gbias[:t_new], (nseq, 1))
        hs, inp_s = _merge_sample(slab_s, prev1, prev2, attn_s, hs, conv_w[l], gn, coef, gbias_rows, gate_b[l],
                                  wb_b, wo_b, t_new, tm=ms)
        hs = _mlp(hs, g2, wu_b, wd_b, tm=ms)
        ks_l.append(k3.reshape(nseq, t_new, N_HEADS, HEAD_DIM))
        vs_l.append(v3.reshape(nseq, t_new, N_HEADS, HEAD_DIM))
        cs_l.append(inp_s.reshape(nseq, t_new, CONV_W)[:, t_new - 2:, :])
        gv_l.append(slab_s[:, COL_GV * 512:(COL_GV + 1) * 512].reshape(nseq, t_new, GMLP_W))

    return (hp.reshape(batch, seq, D_MODEL), hs.reshape(nseq, t_new, D_MODEL),
            jnp.stack(kp_l), jnp.stack(vp_l), jnp.stack(ks_l), jnp.stack(vs_l),
            jnp.stack(cp_l), jnp.stack(cs_l), jnp.stack(gv_l))
```

```python
import functools

import numpy as np
import jax
import jax.numpy as jnp
from jax import lax
from jax.experimental import pallas as pl
from jax.experimental.pallas import tpu as pltpu

D_MODEL = 1024
N_HEADS = 8
HEAD_DIM = 64
ATTN_W = N_HEADS * HEAD_DIM
CONV_W = 512
GMLP_W = 512
GMLP_GROUPS = 8
CHUNK = 128
N_BRANCH = 3
MOBA_BLOCK = 256
MOBA_TOPK = 3
PAGE_SIZE = 128
D_FF = 4 * D_MODEL
EPS = 1e-6
NEG = -1e30
IN_COLS = 3 * ATTN_W + 3 * CONV_W + 2 * GMLP_W + N_BRANCH * D_MODEL
COL_Q, COL_K, COL_V, COL_CB, COL_CC, COL_CH, COL_GU, COL_GV, COL_GL = 0, 1, 2, 3, 4, 5, 6, 7, 8

LANES = 128
SUBLANES = 8
V7X_VMEM_LIMIT = 56 * 1024 * 1024

BF16 = jnp.bfloat16
F32 = jnp.float32
NT_DIMS = (((1,), (1,)), ((), ()))


def _rms_scale(x):
    return lax.rsqrt(jnp.mean(x * x, axis=-1, keepdims=True) + EPS)


def _inproj_kernel(x_ref, g_ref, w_ref, qkg_ref, gmat_ref, o_ref, xn_ref):
    j = pl.program_id(1)

    @pl.when(j == 0)
    def _():
        x = x_ref[...]
        xn_ref[...] = (x * _rms_scale(x) * g_ref[...]).astype(BF16)

    acc = jnp.dot(xn_ref[...], w_ref[...], preferred_element_type=F32)

    @pl.when(j <= COL_K)
    def _():
        sq = acc * acc
        hi = sq.astype(BF16)
        lo = (sq - hi.astype(F32)).astype(BF16)
        gmat = gmat_ref[...]
        ms = jnp.dot(hi, gmat, preferred_element_type=F32) + jnp.dot(lo, gmat, preferred_element_type=F32)
        o_ref[...] = acc * lax.rsqrt(ms + EPS) * qkg_ref[...]

    @pl.when(j > COL_K)
    def _():
        o_ref[...] = acc


def _in_proj(x, g, w_bf, qkg, gmat, tm):
    m = x.shape[0]
    tn = 512
    return pl.pallas_call(
        _inproj_kernel,
        out_shape=jax.ShapeDtypeStruct((m, IN_COLS), F32),
        grid=(m // tm, IN_COLS // tn),
        in_specs=[
            pl.BlockSpec((tm, D_MODEL), lambda i, j: (i, 0)),
            pl.BlockSpec((1, D_MODEL), lambda i, j: (0, 0)),
            pl.BlockSpec((D_MODEL, tn), lambda i, j: (0, j)),
            pl.BlockSpec((None, 1, tn), lambda i, j: (jnp.minimum(j, COL_K), 0, 0)),
            pl.BlockSpec((tn, tn), lambda i, j: (0, 0)),
        ],
        out_specs=pl.BlockSpec((tm, tn), lambda i, j: (i, j)),
        scratch_shapes=[pltpu.VMEM((tm, D_MODEL), BF16)],
        compiler_params=pltpu.CompilerParams(
            dimension_semantics=("parallel", "arbitrary"), vmem_limit_bytes=V7X_VMEM_LIMIT),
        name="in_proj",
    )(x, g, w_bf, qkg, gmat)


def _attn_prompt_kernel(q_ref, k_ref, v_ref, o_ref, kb_ref, vt_ref, km_ref, sel_ref, qs_ref, m_ref, acc_ref,
                        s0_ref, s1_ref, *, group):
    t = pl.program_id(2)
    seq = k_ref.shape[0]
    nblk = seq // MOBA_BLOCK
    lane = lax.broadcasted_iota(jnp.int32, (1, LANES), 1)
    lo_lanes = lane < HEAD_DIM
    lo_rows = lax.broadcasted_iota(jnp.int32, (LANES, 1), 0) < HEAD_DIM

    @pl.when(t == 0)
    def _init():
        kf = k_ref[...]
        kb_ref[...] = kf.astype(BF16)
        km_ref[...] = jnp.sum(kf.reshape(nblk, MOBA_BLOCK, LANES), axis=1) * (1.0 / MOBA_BLOCK)
        for n in range(nblk):
            v_t = v_ref[n * MOBA_BLOCK:(n + 1) * MOBA_BLOCK, :].T
            vt_ref[0, n] = jnp.where(lo_rows, v_t, 1.0).astype(BF16)
            vt_ref[1, n] = jnp.where(lo_rows, 1.0, v_t).astype(BF16)

    q = q_ref[...]
    km = km_ref[...]
    blk = lax.broadcasted_iota(jnp.int32, (nblk, 1), 0)
    krow = lax.broadcasted_iota(jnp.int32, (MOBA_BLOCK, MOBA_BLOCK), 0)
    qcol = lax.broadcasted_iota(jnp.int32, (MOBA_BLOCK, MOBA_BLOCK), 1)
    koff = pl.multiple_of(t * MOBA_BLOCK, MOBA_BLOCK)
    s_refs = (s0_ref, s1_ref)

    def qk_stage(gi, s_ref):
        gi = jnp.minimum(gi, nblk // group - 1)
        for h in range(2):
            qs = qs_ref[h]
            for u in range(group):
                off = pl.multiple_of((gi * group + u) * MOBA_BLOCK, MOBA_BLOCK)
                s_ref[h, u] = lax.dot_general(kb_ref[pl.ds(off, MOBA_BLOCK), :], qs, NT_DIMS,
                                              preferred_element_type=F32)

    def softmax_pv_stage(gi, s_ref):
        for h in range(2):
            m_old = m_ref[h]
            m_new = m_old
            picks = []
            for u in range(group):
                picked = sel_ref[h, pl.ds(gi * group + u, 1), :] > 0.5
                blk_max = jnp.max(s_ref[h, u], axis=0, keepdims=True)
                m_new = jnp.where(picked, jnp.maximum(m_new, blk_max), m_new)
                picks.append(picked)
            acc = jnp.exp(m_old - m_new) * acc_ref[h]
            for u in range(group):
                p = jnp.exp(s_ref[h, u] - jnp.where(picks[u], m_new, -NEG)).astype(BF16)
                acc = acc + jnp.dot(vt_ref[h, gi * group + u], p, preferred_element_type=F32)
            m_ref[h] = m_new
            acc_ref[h] = acc

    gates = []
    s_own = []
    for h in range(2):
        qh = jnp.where(lo_lanes if h == 0 else jnp.logical_not(lo_lanes), q, 0.0)
        gates.append(lax.dot_general(km, qh, NT_DIMS, precision=lax.Precision.HIGHEST, preferred_element_type=F32))
        qs = (qh * (HEAD_DIM ** -0.5)).astype(BF16)
        qs_ref[h] = qs
        s_own.append(lax.dot_general(kb_ref[pl.ds(koff, MOBA_BLOCK), :], qs, NT_DIMS, preferred_element_type=F32))
    qk_stage(0, s_refs[0])

    for h in range(2):
        g = jnp.where(blk < t, gates[h], -jnp.inf)
        rank = jnp.zeros_like(g)
        for mm in range(nblk):
            gm = g[mm:mm + 1, :]
            beats = jnp.logical_or(gm > g, jnp.logical_and(gm == g, blk > mm))
            rank = rank + jnp.where(beats, 1.0, 0.0)
        sel_ref[h] = jnp.where(jnp.logical_and(rank < float(MOBA_TOPK), blk < t), 1.0, 0.0)

        s = jnp.where(krow <= qcol, s_own[h], NEG)
        m0 = jnp.max(s, axis=0, keepdims=True)
        p = jnp.exp(s - m0).astype(BF16)
        m_ref[h] = m0
        acc_ref[h] = jnp.dot(vt_ref[h, t], p, preferred_element_type=F32)

    ngroups = (t + group - 1) // group

    def body(j, carry):
        g0 = 2 * j
        qk_stage(g0 + 1, s_refs[1])
        softmax_pv_stage(g0, s_refs[0])
        qk_stage(g0 + 2, s_refs[0])
        softmax_pv_stage(g0 + 1, s_refs[1])
        return carry

    lax.fori_loop(0, (ngroups + 1) // 2, body, 0)

    a0 = acc_ref[0]
    a1 = acc_ref[1]
    o_t = jnp.where(lo_rows, a0 * (1.0 / a0[HEAD_DIM:HEAD_DIM + 1, :]), a1 * (1.0 / a1[0:1, :]))
    o_ref[...] = o_t.T.astype(o_ref.dtype)


def _attn_prompt(slab, batch, seq, group=2):
    m = slab.shape[0]
    nq = seq // MOBA_BLOCK
    npair = ATTN_W // LANES
    cpb = 512 // LANES
    assert nq % group == 0
    return pl.pallas_call(
        functools.partial(_attn_prompt_kernel, group=group),
        out_shape=jax.ShapeDtypeStruct((m, ATTN_W), BF16),
        grid=(batch, npair, nq),
        in_specs=[
            pl.BlockSpec((MOBA_BLOCK, LANES), lambda b, p, t: (b * nq + t, COL_Q * cpb + p)),
            pl.BlockSpec((seq, LANES), lambda b, p, t: (b, COL_K * cpb + p)),
            pl.BlockSpec((seq, LANES), lambda b, p, t: (b, COL_V * cpb + p)),
        ],
        out_specs=pl.BlockSpec((MOBA_BLOCK, LANES), lambda b, p, t: (b * nq + t, p)),
        scratch_shapes=[
            pltpu.VMEM((seq, LANES), BF16),
            pltpu.VMEM((2, nq, LANES, MOBA_BLOCK), BF16),
            pltpu.VMEM((nq, LANES), F32),
            pltpu.VMEM((2, nq, MOBA_BLOCK), F32),
            pltpu.VMEM((2, MOBA_BLOCK, LANES), BF16),
            pltpu.VMEM((2, 1, MOBA_BLOCK), F32),
            pltpu.VMEM((2, LANES, MOBA_BLOCK), F32),
            pltpu.VMEM((2, group, MOBA_BLOCK, MOBA_BLOCK), F32),
            pltpu.VMEM((2, group, MOBA_BLOCK, MOBA_BLOCK), F32),
        ],
        compiler_params=pltpu.CompilerParams(
            dimension_semantics=("parallel", "parallel", "arbitrary"), vmem_limit_bytes=V7X_VMEM_LIMIT),
        name="attn_prompt",
    )(slab, slab, slab)


def _attn_sample_kernel(pt_ref, q_ref, kn_ref, vn_ref, *refs, n_pages):
    del pt_ref
    kp_refs = refs[:n_pages]
    vp_refs = refs[n_pages:2 * n_pages]
    o_ref = refs[2 * n_pages]
    t_new = q_ref.shape[0]
    rows = t_new * N_HEADS
    prow = PAGE_SIZE * N_HEADS
    pages_per_blk = MOBA_BLOCK // PAGE_SIZE
    n_past = n_pages // pages_per_blk

    q3 = q_ref[...]
    gates = [[None] * n_past for _ in range(t_new)]
    for n in range(n_past):
        km = kp_refs[pages_per_blk * n][...].sum(axis=0)
        for j in range(1, pages_per_blk):
            km = km + kp_refs[pages_per_blk * n + j][...].sum(axis=0)
        km = km * (1.0 / MOBA_BLOCK)
        for t in range(t_new):
            gates[t][n] = jnp.sum(q3[t] * km, axis=1, keepdims=True)
    sel = []
    for n in range(n_past):
        per_token = []
        for t in range(t_new):
            rank = jnp.zeros((N_HEADS, 1), F32)
            for mm in range(n_past):
                if mm == n:
                    continue
                beats = (gates[t][mm] >= gates[t][n]) if mm < n else (gates[t][mm] > gates[t][n])
                rank = rank + jnp.where(beats, 1.0, 0.0)
            per_token.append(jnp.where(rank < float(min(MOBA_TOPK, n_past)), 1.0, 0.0))
        sel.append(jnp.concatenate(per_token, axis=0) > 0.5)

    qb = (q3.reshape(rows, HEAD_DIM) * (HEAD_DIM ** -0.5)).astype(BF16)
    rhead = lax.broadcasted_iota(jnp.int32, (rows, prow), 0) & (N_HEADS - 1)
    chead = lax.broadcasted_iota(jnp.int32, (rows, prow), 1) & (N_HEADS - 1)
    same_head = rhead == chead

    pad = jnp.zeros((LANES - rows, HEAD_DIM), F32)
    kn = jnp.concatenate([kn_ref[...].reshape(rows, HEAD_DIM), pad], axis=0).astype(BF16)
    vn = jnp.concatenate([vn_ref[...].reshape(rows, HEAD_DIM), pad], axis=0).astype(BF16)
    ro = lax.broadcasted_iota(jnp.int32, (rows, LANES), 0)
    co = lax.broadcasted_iota(jnp.int32, (rows, LANES), 1)
    own_ok = (co < rows) & ((ro & (N_HEADS - 1)) == (co & (N_HEADS - 1))) & (
        lax.shift_right_logical(co, 3) <= lax.shift_right_logical(ro, 3))
    s_own = jnp.where(own_ok, lax.dot_general(qb, kn, NT_DIMS, preferred_element_type=F32), NEG)

    s_past = []
    m = jnp.max(s_own, axis=1, keepdims=True)
    for p in range(n_pages):
        kx = kp_refs[p][...].reshape(prow, HEAD_DIM).astype(BF16)
        s = lax.dot_general(qb, kx, NT_DIMS, preferred_element_type=F32)
        s = jnp.where(same_head & sel[p // pages_per_blk], s, NEG)
        s_past.append(s)
        m = jnp.maximum(m, jnp.max(s, axis=1, keepdims=True))

    e = jnp.exp(s_own - m)
    den = jnp.sum(e, axis=1, keepdims=True)
    acc = jnp.dot(e.astype(BF16), vn, preferred_element_type=F32)
    for p in range(n_pages):
        e = jnp.exp(s_past[p] - m)
        den = den + jnp.sum(e, axis=1, keepdims=True)
        vx = vp_refs[p][...].reshape(prow, HEAD_DIM).astype(BF16)
        acc = acc + jnp.dot(e.astype(BF16), vx, preferred_element_type=F32)
    o_ref[...] = (acc / den).reshape(t_new, N_HEADS, HEAD_DIM)


def _attn_sample(layer, pt_flat, q4, k4, v4, cache_k, cache_v, n_pages):
    nseq, t_new = q4.shape[:2]

    def page_spec(p):
        return pl.BlockSpec((None, None, PAGE_SIZE, N_HEADS, HEAD_DIM),
                            lambda s, pt: (layer, pt[s * n_pages + p], 0, 0, 0))

    new_spec = pl.BlockSpec((None, t_new, N_HEADS, HEAD_DIM), lambda s, pt: (s, 0, 0, 0))
    grid_spec = pltpu.PrefetchScalarGridSpec(
        num_scalar_prefetch=1,
        grid=(nseq,),
        in_specs=[new_spec, new_spec, new_spec]
        + [page_spec(p) for p in range(n_pages)] + [page_spec(p) for p in range(n_pages)],
        out_specs=new_spec,
    )
    return pl.pallas_call(
        functools.partial(_attn_sample_kernel, n_pages=n_pages),
        out_shape=jax.ShapeDtypeStruct((nseq, t_new, N_HEADS, HEAD_DIM), F32),
        grid_spec=grid_spec,
        compiler_params=pltpu.CompilerParams(
            dimension_semantics=("arbitrary",), vmem_limit_bytes=V7X_VMEM_LIMIT),
        name="attn_sample",
    )(pt_flat, q4, k4, v4, *([cache_k] * n_pages), *([cache_v] * n_pages))


def _merge_tail(conv, gm, attn_ref, gl_refs, gb_ref, wb_ref, wo_ref, x_ref, y_ref):
    branches = (attn_ref[...].astype(BF16), conv.astype(BF16), gm.astype(BF16))
    merged = None
    for i in range(N_BRANCH):
        gate = jax.nn.sigmoid(gl_refs[i][...] + gb_ref[i:i + 1, :])
        term = gate * jnp.dot(branches[i], wb_ref[i], preferred_element_type=F32)
        merged = term if merged is None else merged + term
    y_ref[...] = x_ref[...] + jnp.dot(merged.astype(BF16), wo_ref[...], preferred_element_type=F32)


def _conv_taps(inp, m1, m2, cw_ref):
    return m2 * cw_ref[0:1, :] + m1 * cw_ref[1:2, :] + inp * cw_ref[2:3, :]


def _merge_prompt_kernel(cb_ref, cc_ref, ch_ref, gu_ref, gv_ref, gl0_ref, gl1_ref, gl2_ref, ccp_ref, chp_ref,
                         attn_ref, x_ref, cw_ref, gn_ref, ws_ref, gbias_ref, gb_ref, wb_ref, wo_ref,
                         y_ref, tail_ref, *, tiles_per_seq):
    i = pl.program_id(0)
    tm = cb_ref.shape[0]
    inp = cc_ref[...] * ch_ref[...]
    prev = ccp_ref[...] * chp_ref[...]
    prev = jnp.where(i % tiles_per_seq == 0, 0.0, prev)
    p0 = prev[6:7, :]
    p1 = prev[7:8, :]
    rowi = lax.broadcasted_iota(jnp.int32, (tm, 1), 0)
    m1 = jnp.where(rowi == 0, p1, pltpu.roll(inp, 1, axis=0))
    m2 = jnp.where(rowi == 0, p0, jnp.where(rowi == 1, p1, pltpu.roll(inp, 2, axis=0)))
    conv = cb_ref[...] * _conv_taps(inp, m1, m2, cw_ref)
    tail_ref[...] = inp[tm - 8:tm, :]

    gv = gv_ref[...]
    vn = gv * _rms_scale(gv) * gn_ref[...]
    lane = lax.broadcasted_iota(jnp.int32, (1, LANES), 1)
    lo_half = lane < (GMLP_W // GMLP_GROUPS)
    wrow = lax.broadcasted_iota(jnp.int32, (CHUNK, 2 * CHUNK), 0)
    wcol = lax.broadcasted_iota(jnp.int32, (CHUNK, 2 * CHUNK), 1) & (CHUNK - 1)
    cols = []
    for c in range(GMLP_W // LANES):
        w = jnp.where(wcol <= wrow, ws_ref[c], 0.0).astype(BF16)
        vc = vn[:, c * LANES:(c + 1) * LANES]
        chunks = []
        for k in range(tm // CHUNK):
            vck = vc[k * CHUNK:(k + 1) * CHUNK, :]
            rhs = jnp.concatenate([jnp.where(lo_half, vck, 0.0), jnp.where(lo_half, 0.0, vck)], axis=0)
            mixed = jnp.dot(w, rhs.astype(BF16), preferred_element_type=F32)
            chunks.append(mixed + gbias_ref[:, c * LANES:(c + 1) * LANES])
        cols.append(jnp.concatenate(chunks, axis=0))
    gm = gu_ref[...] * jnp.concatenate(cols, axis=1)

    _merge_tail(conv, gm, attn_ref, (gl0_ref, gl1_ref, gl2_ref), gb_ref, wb_ref, wo_ref, x_ref, y_ref)


def _merge_sample_kernel(cb_ref, cc_ref, ch_ref, gu_ref, gv_ref, gl0_ref, gl1_ref, gl2_ref, prev1_ref, prev2_ref,
                         attn_ref, x_ref, cw_ref, gn_ref, coef_ref, gbias_ref, gb_ref, wb_ref, wo_ref,
                         y_ref, inp_ref, *, t_new):
    tm = cb_ref.shape[0]
    inp = cc_ref[...] * ch_ref[...]
    tpos = lax.broadcasted_iota(jnp.int32, (tm, 1), 0) & (t_new - 1)
    m1 = jnp.where(tpos == 0, prev1_ref[...], pltpu.roll(inp, 1, axis=0))
    m2 = jnp.where(tpos < 2, prev2_ref[...], pltpu.roll(inp, 2, axis=0))
    conv = cb_ref[...] * _conv_taps(inp, m1, m2, cw_ref)
    inp_ref[...] = inp

    gv = gv_ref[...]
    vn = gv * _rms_scale(gv) * gn_ref[...]
    mixed = coef_ref[0] * vn + gbias_ref[...]
    for d in range(1, t_new):
        mixed = mixed + coef_ref[d] * pltpu.roll(vn, d, axis=0)
    gm = gu_ref[...] * mixed

    _merge_tail(conv, gm, attn_ref, (gl0_ref, gl1_ref, gl2_ref), gb_ref, wb_ref, wo_ref, x_ref, y_ref)


def _slab_specs(tm):
    col = lambda c: pl.BlockSpec((tm, 512), lambda i, c=c: (i, c))
    gl = lambda g: pl.BlockSpec((tm, D_MODEL), lambda i, g=g: (i, COL_GL // 2 + g))
    return [col(COL_CB), col(COL_CC), col(COL_CH), col(COL_GU), col(COL_GV), gl(0), gl(1), gl(2)]


def _full(shape):
    return pl.BlockSpec(shape, lambda i: (0,) * len(shape))


def _merge_prompt(slab, attn, x, cw, gn, ws_cat, gbias, gb, wb, wo, seq, tm):
    m = slab.shape[0]
    ntiles = m // tm
    halo = lambda c: pl.BlockSpec((8, 512), lambda i, c=c: (jnp.maximum(i * (tm // 8) - 1, 0), c))
    y, tail = pl.pallas_call(
        functools.partial(_merge_prompt_kernel, tiles_per_seq=seq // tm),
        out_shape=(jax.ShapeDtypeStruct((m, D_MODEL), F32), jax.ShapeDtypeStruct((ntiles, 8, CONV_W), F32)),
        grid=(ntiles,),
        in_specs=_slab_specs(tm) + [halo(COL_CC), halo(COL_CH),
                                    pl.BlockSpec((tm, ATTN_W), lambda i: (i, 0)),
                                    pl.BlockSpec((tm, D_MODEL), lambda i: (i, 0)),
                                    _full(cw.shape), _full(gn.shape), _full(ws_cat.shape), _full(gbias.shape),
                                    _full(gb.shape), _full(wb.shape), _full(wo.shape)],
        out_specs=(pl.BlockSpec((tm, D_MODEL), lambda i: (i, 0)),
                   pl.BlockSpec((None, 8, CONV_W), lambda i: (i, 0, 0))),
        compiler_params=pltpu.CompilerParams(
            dimension_semantics=("parallel",), vmem_limit_bytes=V7X_VMEM_LIMIT),
        name="merge_prompt",
    )(*([slab] * 10), attn, x, cw, gn, ws_cat, gbias, gb, wb, wo)
    return y, tail


def _merge_sample(slab, prev1, prev2, attn, x, cw, gn, coef, gbias_rows, gb, wb, wo, t_new, tm):
    m = slab.shape[0]
    rows = lambda w: pl.BlockSpec((tm, w), lambda i: (i, 0))
    y, inp = pl.pallas_call(
        functools.partial(_merge_sample_kernel, t_new=t_new),
        out_shape=(jax.ShapeDtypeStruct((m, D_MODEL), F32), jax.ShapeDtypeStruct((m, CONV_W), F32)),
        grid=(m // tm,),
        in_specs=_slab_specs(tm) + [rows(CONV_W), rows(CONV_W), rows(ATTN_W), rows(D_MODEL),
                                    _full(cw.shape), _full(gn.shape),
                                    pl.BlockSpec((t_new, tm, GMLP_W), lambda i: (0, i, 0)), rows(GMLP_W),
                                    _full(gb.shape), _full(wb.shape), _full(wo.shape)],
        out_specs=(rows(D_MODEL), rows(CONV_W)),
        compiler_params=pltpu.CompilerParams(
            dimension_semantics=("parallel",), vmem_limit_bytes=V7X_VMEM_LIMIT),
        name="merge_sample",
    )(*([slab] * 8), prev1, prev2, attn, x, cw, gn, coef, gbias_rows, gb, wb, wo)
    return y, inp


def _mlp_kernel(x_ref, g_ref, wu_ref, wd_ref, o_ref, xn_ref, acc_ref):
    f = pl.program_id(1)

    @pl.when(f == 0)
    def _():
        x = x_ref[...]
        xn_ref[...] = (x * _rms_scale(x) * g_ref[...]).astype(BF16)
        acc_ref[...] = jnp.zeros_like(acc_ref)

    h = jnp.maximum(jnp.dot(xn_ref[...], wu_ref[...], preferred_element_type=F32), 0.0)
    acc_ref[...] += jnp.dot((h * h).astype(BF16), wd_ref[...], preferred_element_type=F32)

    @pl.when(f == pl.num_programs(1) - 1)
    def _():
        o_ref[...] = x_ref[...] + acc_ref[...]


def _mlp(x, g, wu, wd, tm):
    m = x.shape[0]
    tf = 1024
    return pl.pallas_call(
        _mlp_kernel,
        out_shape=jax.ShapeDtypeStruct((m, D_MODEL), F32),
        grid=(m // tm, D_FF // tf),
        in_specs=[
            pl.BlockSpec((tm, D_MODEL), lambda i, f: (i, 0)),
            pl.BlockSpec((1, D_MODEL), lambda i, f: (0, 0)),
            pl.BlockSpec((D_MODEL, tf), lambda i, f: (0, f)),
            pl.BlockSpec((tf, D_MODEL), lambda i, f: (f, 0)),
        ],
        out_specs=pl.BlockSpec((tm, D_MODEL), lambda i, f: (i, 0)),
        scratch_shapes=[pltpu.VMEM((tm, D_MODEL), BF16), pltpu.VMEM((tm, D_MODEL), F32)],
        compiler_params=pltpu.CompilerParams(
            dimension_semantics=("parallel", "arbitrary"), vmem_limit_bytes=V7X_VMEM_LIMIT),
        name="mlp",
    )(x, g, wu, wd)


def _head_mean_matrix():
    idx = np.arange(512) // HEAD_DIM
    return jnp.asarray((idx[:, None] == idx[None, :]).astype(np.float32) / HEAD_DIM, dtype=BF16)


def kernel(x_prompt, x_sample, cache_k, cache_v, state_conv, page_table, norm1_g, w_in, q_norm_g, k_norm_g,
           conv_w, gmlp_norm_g, gmlp_ws, gmlp_b, gate_b, w_branch, w_out, norm2_g, w_up, w_down):
    batch, seq, _ = x_prompt.shape
    nseq, t_new, _ = x_sample.shape
    depth = w_in.shape[0]
    n_pages = page_table.shape[1]
    gdim = GMLP_W // GMLP_GROUPS
    assert t_new == 4 and seq % MOBA_BLOCK == 0 and (n_pages * PAGE_SIZE) % MOBA_BLOCK == 0

    hp = x_prompt.reshape(batch * seq, D_MODEL)
    hs = x_sample.reshape(nseq * t_new, D_MODEL)
    ms = hs.shape[0]
    pt_flat = page_table.reshape(-1).astype(jnp.int32)
    gmat = _head_mean_matrix()

    kp_l, vp_l, ks_l, vs_l, cp_l, cs_l, gv_l = [], [], [], [], [], [], []
    for l in range(depth):
        w_in_b = w_in[l].astype(BF16)
        wb_b = w_branch[l].astype(BF16)
        wo_b = w_out[l].astype(BF16)
        wu_b = w_up[l].astype(BF16)
        wd_b = w_down[l].astype(BF16)
        g1 = norm1_g[l].reshape(1, D_MODEL)
        g2 = norm2_g[l].reshape(1, D_MODEL)
        qkg = jnp.stack([jnp.tile(q_norm_g[l], N_HEADS), jnp.tile(k_norm_g[l], N_HEADS)]).reshape(2, 1, ATTN_W)
        gn = gmlp_norm_g[l].reshape(1, GMLP_W)
        ws = gmlp_ws[l]
        ws_cat = ws.reshape(GMLP_GROUPS // 2, 2, CHUNK, CHUNK).transpose(0, 2, 1, 3).reshape(
            GMLP_GROUPS // 2, CHUNK, 2 * CHUNK)
        gbias = jnp.repeat(gmlp_b[l].T, gdim, axis=1)

        slab = _in_proj(hp, g1, w_in_b, qkg, gmat, tm=2048)
        attn = _attn_prompt(slab, batch, seq)
        tm_p = 256
        hp, tail = _merge_prompt(slab, attn, hp, conv_w[l], gn, ws_cat, gbias, gate_b[l], wb_b, wo_b, seq, tm_p)
        hp = _mlp(hp, g2, wu_b, wd_b, tm=1024)
        kp_l.append(slab[:, COL_K * 512:(COL_K + 1) * 512].reshape(batch, seq, N_HEADS, HEAD_DIM))
        vp_l.append(slab[:, COL_V * 512:(COL_V + 1) * 512].reshape(batch, seq, N_HEADS, HEAD_DIM))
        cp_l.append(tail.reshape(batch, seq // tm_p, 8, CONV_W)[:, -1, 6:8, :])

        slab_s = _in_proj(hs, g1, w_in_b, qkg, gmat, tm=ms)
        q4 = slab_s[:, COL_Q * 512:(COL_Q + 1) * 512].reshape(nseq, t_new, N_HEADS, HEAD_DIM)
        k4 = slab_s[:, COL_K * 512:(COL_K + 1) * 512].reshape(nseq, t_new, N_HEADS, HEAD_DIM)
        v4 = slab_s[:, COL_V * 512:(COL_V + 1) * 512].reshape(nseq, t_new, N_HEADS, HEAD_DIM)
        attn_s = _attn_sample(l, pt_flat, q4, k4, v4, cache_k, cache_v, n_pages).reshape(ms, ATTN_W)
        st = state_conv[l]
        zero = jnp.zeros((nseq, 1, CONV_W), F32)
        prev1 = jnp.concatenate([st[:, 1:2], zero, zero, zero], axis=1).reshape(ms, CONV_W)
        prev2 = jnp.concatenate([st[:, 0:1], st[:, 1:2], zero, zero], axis=1).reshape(ms, CONV_W)
        tt = np.arange(t_new)
        coef_rows = []
        for d in range(t_new):
            wd_sel = ws[:, tt, np.maximum(tt - d, 0)] * jnp.asarray((tt >= d).astype(np.float32))
            coef_rows.append(jnp.repeat(wd_sel.T, gdim, axis=1))
        coef = jnp.tile(jnp.stack(coef_rows), (1, nseq, 1))
        gbias_rows = jnp.tile(gbias[:t_new], (nseq, 1))
        hs, inp_s = _merge_sample(slab_s, prev1, prev2, attn_s, hs, conv_w[l], gn, coef, gbias_rows, gate_b[l],
                                  wb_b, wo_b, t_new, tm=ms)
        hs = _mlp(hs, g2, wu_b, wd_b, tm=ms)
        ks_l.append(k4)
        vs_l.append(v4)
        cs_l.append(inp_s.reshape(nseq, t_new, CONV_W)[:, t_new - 2:, :])
        gv_l.append(slab_s[:, COL_GV * 512:(COL_GV + 1) * 512].reshape(nseq, t_new, GMLP_W))

    return (hp.reshape(batch, seq, D_MODEL), hs.reshape(nseq, t_new, D_MODEL),
            jnp.stack(kp_l), jnp.stack(vp_l), jnp.stack(ks_l), jnp.stack(vs_l),
            jnp.stack(cp_l), jnp.stack(cs_l), jnp.stack(gv_l))
```

```python
import functools

import numpy as np
import jax
import jax.numpy as jnp
from jax import lax
from jax.experimental import pallas as pl
from jax.experimental.pallas import tpu as pltpu

D_MODEL = 1024
N_HEADS = 8
HEAD_DIM = 64
ATTN_W = N_HEADS * HEAD_DIM
CONV_W = 512
GMLP_W = 512
GMLP_GROUPS = 8
CHUNK = 128
N_BRANCH = 3
MOBA_BLOCK = 256
MOBA_TOPK = 3
PAGE_SIZE = 128
D_FF = 4 * D_MODEL
EPS = 1e-6
NEG = -1e30
IN_COLS = 3 * ATTN_W + 3 * CONV_W + 2 * GMLP_W + N_BRANCH * D_MODEL
COL_Q, COL_K, COL_V, COL_CB, COL_CC, COL_CH, COL_GU, COL_GV, COL_GL = 0, 1, 2, 3, 4, 5, 6, 7, 8

LANES = 128
SUBLANES = 8
V7X_VMEM_LIMIT = 56 * 1024 * 1024

BF16 = jnp.bfloat16
F32 = jnp.float32
NT_DIMS = (((1,), (1,)), ((), ()))


def _rms_scale(x):
    return lax.rsqrt(jnp.mean(x * x, axis=-1, keepdims=True) + EPS)


def _inproj_kernel(x_ref, g_ref, w_ref, qkg_ref, gmat_ref, o_ref, xn_ref):
    j = pl.program_id(1)

    @pl.when(j == 0)
    def _():
        x = x_ref[...]
        xn_ref[...] = (x * _rms_scale(x) * g_ref[...]).astype(BF16)

    acc = jnp.dot(xn_ref[...], w_ref[...], preferred_element_type=F32)

    @pl.when(j <= COL_K)
    def _():
        sq = acc * acc
        hi = sq.astype(BF16)
        lo = (sq - hi.astype(F32)).astype(BF16)
        gmat = gmat_ref[...]
        ms = jnp.dot(hi, gmat, preferred_element_type=F32) + jnp.dot(lo, gmat, preferred_element_type=F32)
        o_ref[...] = acc * lax.rsqrt(ms + EPS) * qkg_ref[...]

    @pl.when(j > COL_K)
    def _():
        o_ref[...] = acc


def _in_proj(x, g, w_bf, qkg, gmat, tm):
    m = x.shape[0]
    tn = 512
    return pl.pallas_call(
        _inproj_kernel,
        out_shape=jax.ShapeDtypeStruct((m, IN_COLS), F32),
        grid=(m // tm, IN_COLS // tn),
        in_specs=[
            pl.BlockSpec((tm, D_MODEL), lambda i, j: (i, 0)),
            pl.BlockSpec((1, D_MODEL), lambda i, j: (0, 0)),
            pl.BlockSpec((D_MODEL, tn), lambda i, j: (0, j)),
            pl.BlockSpec((None, 1, tn), lambda i, j: (jnp.minimum(j, COL_K), 0, 0)),
            pl.BlockSpec((tn, tn), lambda i, j: (0, 0)),
        ],
        out_specs=pl.BlockSpec((tm, tn), lambda i, j: (i, j)),
        scratch_shapes=[pltpu.VMEM((tm, D_MODEL), BF16)],
        compiler_params=pltpu.CompilerParams(
            dimension_semantics=("parallel", "arbitrary"), vmem_limit_bytes=V7X_VMEM_LIMIT),
        name="in_proj",
    )(x, g, w_bf, qkg, gmat)


def _attn_prompt_kernel(q_ref, k_ref, v_ref, o_ref, kto_ref, vto_ref, kb_ref, vt_ref, km_ref, sel_ref, qs_ref, m_ref, acc_ref,
                        s0_ref, s1_ref, *, group):
    t = pl.program_id(2)
    seq = k_ref.shape[0]
    nblk = seq // MOBA_BLOCK
    lane = lax.broadcasted_iota(jnp.int32, (1, LANES), 1)
    lo_lanes = lane < HEAD_DIM
    lo_rows = lax.broadcasted_iota(jnp.int32, (LANES, 1), 0) < HEAD_DIM

    @pl.when(t == 0)
    def _init():
        kf = k_ref[...]
        kb_ref[...] = kf.astype(BF16)
        km_ref[...] = jnp.sum(kf.reshape(nblk, MOBA_BLOCK, LANES), axis=1) * (1.0 / MOBA_BLOCK)
        for n in range(nblk):
            v_t = v_ref[n * MOBA_BLOCK:(n + 1) * MOBA_BLOCK, :].T
            kto_ref[:, n * MOBA_BLOCK:(n + 1) * MOBA_BLOCK] = kf[n * MOBA_BLOCK:(n + 1) * MOBA_BLOCK, :].T
            vto_ref[:, n * MOBA_BLOCK:(n + 1) * MOBA_BLOCK] = v_t
            vt_ref[0, n] = jnp.where(lo_rows, v_t, 1.0).astype(BF16)
            vt_ref[1, n] = jnp.where(lo_rows, 1.0, v_t).astype(BF16)

    q = q_ref[...]
    km = km_ref[...]
    blk = lax.broadcasted_iota(jnp.int32, (nblk, 1), 0)
    krow = lax.broadcasted_iota(jnp.int32, (MOBA_BLOCK, MOBA_BLOCK), 0)
    qcol = lax.broadcasted_iota(jnp.int32, (MOBA_BLOCK, MOBA_BLOCK), 1)
    koff = pl.multiple_of(t * MOBA_BLOCK, MOBA_BLOCK)
    s_refs = (s0_ref, s1_ref)

    def qk_stage(gi, s_ref):
        gi = jnp.minimum(gi, nblk // group - 1)
        for h in range(2):
            qs = qs_ref[h]
            for u in range(group):
                off = pl.multiple_of((gi * group + u) * MOBA_BLOCK, MOBA_BLOCK)
                s_ref[h, u] = lax.dot_general(kb_ref[pl.ds(off, MOBA_BLOCK), :], qs, NT_DIMS,
                                              preferred_element_type=F32)

    def softmax_pv_stage(gi, s_ref):
        for h in range(2):
            m_old = m_ref[h]
            m_new = m_old
            picks = []
            for u in range(group):
                picked = sel_ref[h, pl.ds(gi * group + u, 1), :] > 0.5
                blk_max = jnp.max(s_ref[h, u], axis=0, keepdims=True)
                m_new = jnp.where(picked, jnp.maximum(m_new, blk_max), m_new)
                picks.append(picked)
            acc = jnp.exp(m_old - m_new) * acc_ref[h]
            for u in range(group):
                p = jnp.exp(s_ref[h, u] - jnp.where(picks[u], m_new, -NEG)).astype(BF16)
                acc = acc + jnp.dot(vt_ref[h, gi * group + u], p, preferred_element_type=F32)
            m_ref[h] = m_new
            acc_ref[h] = acc

    gates = []
    s_own = []
    for h in range(2):
        qh = jnp.where(lo_lanes if h == 0 else jnp.logical_not(lo_lanes), q, 0.0)
        gates.append(lax.dot_general(km, qh, NT_DIMS, precision=lax.Precision.HIGHEST, preferred_element_type=F32))
        qs = (qh * (HEAD_DIM ** -0.5)).astype(BF16)
        qs_ref[h] = qs
        s_own.append(lax.dot_general(kb_ref[pl.ds(koff, MOBA_BLOCK), :], qs, NT_DIMS, preferred_element_type=F32))
    qk_stage(0, s_refs[0])

    for h in range(2):
        g = jnp.where(blk < t, gates[h], -jnp.inf)
        rank = jnp.zeros_like(g)
        for mm in range(nblk):
            gm = g[mm:mm + 1, :]
            beats = jnp.logical_or(gm > g, jnp.logical_and(gm == g, blk > mm))
            rank = rank + jnp.where(beats, 1.0, 0.0)
        sel_ref[h] = jnp.where(jnp.logical_and(rank < float(MOBA_TOPK), blk < t), 1.0, 0.0)

        s = jnp.where(krow <= qcol, s_own[h], NEG)
        m0 = jnp.max(s, axis=0, keepdims=True)
        p = jnp.exp(s - m0).astype(BF16)
        m_ref[h] = m0
        acc_ref[h] = jnp.dot(vt_ref[h, t], p, preferred_element_type=F32)

    ngroups = (t + group - 1) // group

    def body(j, carry):
        g0 = 2 * j
        qk_stage(g0 + 1, s_refs[1])
        softmax_pv_stage(g0, s_refs[0])
        qk_stage(g0 + 2, s_refs[0])
        softmax_pv_stage(g0 + 1, s_refs[1])
        return carry

    lax.fori_loop(0, (ngroups + 1) // 2, body, 0)

    a0 = acc_ref[0]
    a1 = acc_ref[1]
    o_t = jnp.where(lo_rows, a0 * (1.0 / a0[HEAD_DIM:HEAD_DIM + 1, :]), a1 * (1.0 / a1[0:1, :]))
    o_ref[...] = o_t.T.astype(o_ref.dtype)


def _attn_prompt(slab, batch, seq, group=2):
    m = slab.shape[0]
    nq = seq // MOBA_BLOCK
    npair = ATTN_W // LANES
    cpb = 512 // LANES
    assert nq % group == 0
    return pl.pallas_call(
        functools.partial(_attn_prompt_kernel, group=group),
        out_shape=(jax.ShapeDtypeStruct((m, ATTN_W), BF16),
                   jax.ShapeDtypeStruct((batch, ATTN_W, seq), F32),
                   jax.ShapeDtypeStruct((batch, ATTN_W, seq), F32)),
        grid=(batch, npair, nq),
        in_specs=[
            pl.BlockSpec((MOBA_BLOCK, LANES), lambda b, p, t: (b * nq + t, COL_Q * cpb + p)),
            pl.BlockSpec((seq, LANES), lambda b, p, t: (b, COL_K * cpb + p)),
            pl.BlockSpec((seq, LANES), lambda b, p, t: (b, COL_V * cpb + p)),
        ],
        out_specs=(pl.BlockSpec((MOBA_BLOCK, LANES), lambda b, p, t: (b * nq + t, p)),
                   pl.BlockSpec((None, LANES, seq), lambda b, p, t: (b, p, 0)),
                   pl.BlockSpec((None, LANES, seq), lambda b, p, t: (b, p, 0))),
        scratch_shapes=[
            pltpu.VMEM((seq, LANES), BF16),
            pltpu.VMEM((2, nq, LANES, MOBA_BLOCK), BF16),
            pltpu.VMEM((nq, LANES), F32),
            pltpu.VMEM((2, nq, MOBA_BLOCK), F32),
            pltpu.VMEM((2, MOBA_BLOCK, LANES), BF16),
            pltpu.VMEM((2, 1, MOBA_BLOCK), F32),
            pltpu.VMEM((2, LANES, MOBA_BLOCK), F32),
            pltpu.VMEM((2, group, MOBA_BLOCK, MOBA_BLOCK), F32),
            pltpu.VMEM((2, group, MOBA_BLOCK, MOBA_BLOCK), F32),
        ],
        compiler_params=pltpu.CompilerParams(
            dimension_semantics=("parallel", "parallel", "arbitrary"), vmem_limit_bytes=V7X_VMEM_LIMIT),
        name="attn_prompt",
    )(slab, slab, slab)


def _attn_sample_kernel(pt_ref, q_ref, kn_ref, vn_ref, *refs, n_pages):
    del pt_ref
    kp_refs = refs[:n_pages]
    vp_refs = refs[n_pages:2 * n_pages]
    o_ref = refs[2 * n_pages]
    t_new = q_ref.shape[0]
    rows = t_new * N_HEADS
    pages_per_blk = MOBA_BLOCK // PAGE_SIZE
    n_past = n_pages // pages_per_blk
    assert n_past <= LANES

    q = q_ref[...]
    kn = kn_ref[...]
    vn = vn_ref[...]
    head_of_lane = lax.shift_right_logical(lax.broadcasted_iota(jnp.int32, (N_HEADS, ATTN_W), 1), 6)
    hm = head_of_lane == lax.broadcasted_iota(jnp.int32, (N_HEADS, ATTN_W), 0)
    qf = jnp.concatenate(
        [jnp.where(hm, jnp.broadcast_to(q[t:t + 1, :], (N_HEADS, ATTN_W)), 0.0) for t in range(t_new)], axis=0)
    hm_rows = jnp.concatenate([hm] * t_new, axis=0)
    trow = lax.shift_right_logical(lax.broadcasted_iota(jnp.int32, (rows, 1), 0), 3)
    lane = lax.broadcasted_iota(jnp.int32, (1, LANES), 1)

    kmean = jnp.zeros((ATTN_W, LANES), F32)
    for n in range(n_past):
        ksum = kp_refs[pages_per_blk * n][...]
        for j in range(1, pages_per_blk):
            ksum = ksum + kp_refs[pages_per_blk * n + j][...]
        km = jnp.sum(ksum, axis=1, keepdims=True) * (1.0 / MOBA_BLOCK)
        kmean = jnp.where(lane == n, km, kmean)
    gate = jnp.dot(qf, kmean, precision=lax.Precision.HIGHEST, preferred_element_type=F32)
    g = jnp.where(lane < n_past, gate, -jnp.inf)
    rank = jnp.zeros_like(g)
    for mm in range(n_past):
        gm = jnp.broadcast_to(g[:, mm:mm + 1], g.shape)
        beats = jnp.logical_or(gm > g, jnp.logical_and(gm == g, lane > mm))
        rank = rank + jnp.where(beats, 1.0, 0.0)
    sel = jnp.where(rank < float(min(MOBA_TOPK, n_past)), 1.0, 0.0)

    qs = qf * (HEAD_DIM ** -0.5)
    qb = qs.astype(BF16)
    s_past = []
    for p in range(n_pages):
        n = p // pages_per_blk
        s = jnp.dot(qb, kp_refs[p][...].astype(BF16), preferred_element_type=F32)
        s_past.append(jnp.where(sel[:, n:n + 1] > 0.5, s, NEG))
    s_own = []
    for i in range(t_new):
        s = jnp.sum(qs * kn[i:i + 1, :], axis=1, keepdims=True)
        s_own.append(jnp.where(trow >= i, s, NEG))

    m = s_own[0]
    for s in s_own[1:]:
        m = jnp.maximum(m, s)
    for s in s_past:
        m = jnp.maximum(m, jnp.max(s, axis=1, keepdims=True))

    acc = jnp.zeros((rows, ATTN_W), F32)
    den = jnp.zeros((rows, 1), F32)
    for p in range(n_pages):
        e = jnp.exp(s_past[p] - m)
        den = den + jnp.sum(e, axis=1, keepdims=True)
        acc = acc + lax.dot_general(e.astype(BF16), vp_refs[p][...].astype(BF16), NT_DIMS,
                                    preferred_element_type=F32)
    for i in range(t_new):
        e = jnp.exp(s_own[i] - m)
        den = den + e
        acc = acc + e * vn[i:i + 1, :]
    o = jnp.where(hm_rows, acc / den, 0.0)
    o_ref[...] = jnp.sum(o.reshape(t_new, N_HEADS, ATTN_W), axis=1)


def _attn_sample(layer, pt_flat, q3, k3, v3, cache_kt, cache_vt, n_pages):
    nseq, t_new, _ = q3.shape

    def page_spec(p):
        return pl.BlockSpec((None, None, ATTN_W, PAGE_SIZE),
                            lambda s, pt: (layer, pt[s * n_pages + p], 0, 0))

    new_spec = pl.BlockSpec((None, t_new, ATTN_W), lambda s, pt: (s, 0, 0))
    grid_spec = pltpu.PrefetchScalarGridSpec(
        num_scalar_prefetch=1,
        grid=(nseq,),
        in_specs=[new_spec, new_spec, new_spec]
        + [page_spec(p) for p in range(n_pages)] + [page_spec(p) for p in range(n_pages)],
        out_specs=new_spec,
    )
    return pl.pallas_call(
        functools.partial(_attn_sample_kernel, n_pages=n_pages),
        out_shape=jax.ShapeDtypeStruct((nseq, t_new, ATTN_W), F32),
        grid_spec=grid_spec,
        compiler_params=pltpu.CompilerParams(
            dimension_semantics=("arbitrary",), vmem_limit_bytes=V7X_VMEM_LIMIT),
        name="attn_sample",
    )(pt_flat, q3, k3, v3, *([cache_kt] * n_pages), *([cache_vt] * n_pages))


def _merge_tail(conv, gm, attn_ref, gl_refs, gb_ref, wb_ref, wo_ref, x_ref, y_ref):
    branches = (attn_ref[...].astype(BF16), conv.astype(BF16), gm.astype(BF16))
    merged = None
    for i in range(N_BRANCH):
        gate = jax.nn.sigmoid(gl_refs[i][...] + gb_ref[i:i + 1, :])
        term = gate * jnp.dot(branches[i], wb_ref[i], preferred_element_type=F32)
        merged = term if merged is None else merged + term
    y_ref[...] = x_ref[...] + jnp.dot(merged.astype(BF16), wo_ref[...], preferred_element_type=F32)


def _conv_taps(inp, m1, m2, cw_ref):
    return m2 * cw_ref[0:1, :] + m1 * cw_ref[1:2, :] + inp * cw_ref[2:3, :]


def _merge_prompt_kernel(cb_ref, cc_ref, ch_ref, gu_ref, gv_ref, gl0_ref, gl1_ref, gl2_ref, ccp_ref, chp_ref,
                         attn_ref, x_ref, cw_ref, gn_ref, ws_ref, gbias_ref, gb_ref, wb_ref, wo_ref,
                         y_ref, tail_ref, *, tiles_per_seq):
    i = pl.program_id(0)
    tm = cb_ref.shape[0]
    inp = cc_ref[...] * ch_ref[...]
    prev = ccp_ref[...] * chp_ref[...]
    prev = jnp.where(i % tiles_per_seq == 0, 0.0, prev)
    p0 = prev[6:7, :]
    p1 = prev[7:8, :]
    rowi = lax.broadcasted_iota(jnp.int32, (tm, 1), 0)
    m1 = jnp.where(rowi == 0, p1, pltpu.roll(inp, 1, axis=0))
    m2 = jnp.where(rowi == 0, p0, jnp.where(rowi == 1, p1, pltpu.roll(inp, 2, axis=0)))
    conv = cb_ref[...] * _conv_taps(inp, m1, m2, cw_ref)
    tail_ref[...] = inp[tm - 8:tm, :]

    gv = gv_ref[...]
    vn = gv * _rms_scale(gv) * gn_ref[...]
    lane = lax.broadcasted_iota(jnp.int32, (1, LANES), 1)
    lo_half = lane < (GMLP_W // GMLP_GROUPS)
    wrow = lax.broadcasted_iota(jnp.int32, (CHUNK, 2 * CHUNK), 0)
    wcol = lax.broadcasted_iota(jnp.int32, (CHUNK, 2 * CHUNK), 1) & (CHUNK - 1)
    cols = []
    for c in range(GMLP_W // LANES):
        w = jnp.where(wcol <= wrow, ws_ref[c], 0.0).astype(BF16)
        vc = vn[:, c * LANES:(c + 1) * LANES]
        chunks = []
        for k in range(tm // CHUNK):
            vck = vc[k * CHUNK:(k + 1) * CHUNK, :]
            rhs = jnp.concatenate([jnp.where(lo_half, vck, 0.0), jnp.where(lo_half, 0.0, vck)], axis=0)
            mixed = jnp.dot(w, rhs.astype(BF16), preferred_element_type=F32)
            chunks.append(mixed + gbias_ref[:, c * LANES:(c + 1) * LANES])
        cols.append(jnp.concatenate(chunks, axis=0))
    gm = gu_ref[...] * jnp.concatenate(cols, axis=1)

    _merge_tail(conv, gm, attn_ref, (gl0_ref, gl1_ref, gl2_ref), gb_ref, wb_ref, wo_ref, x_ref, y_ref)


def _merge_sample_kernel(cb_ref, cc_ref, ch_ref, gu_ref, gv_ref, gl0_ref, gl1_ref, gl2_ref, prev1_ref, prev2_ref,
                         attn_ref, x_ref, cw_ref, gn_ref, coef_ref, gbias_ref, gb_ref, wb_ref, wo_ref,
                         y_ref, inp_ref, *, t_new):
    tm = cb_ref.shape[0]
    inp = cc_ref[...] * ch_ref[...]
    tpos = lax.broadcasted_iota(jnp.int32, (tm, 1), 0) & (t_new - 1)
    m1 = jnp.where(tpos == 0, prev1_ref[...], pltpu.roll(inp, 1, axis=0))
    m2 = jnp.where(tpos < 2, prev2_ref[...], pltpu.roll(inp, 2, axis=0))
    conv = cb_ref[...] * _conv_taps(inp, m1, m2, cw_ref)
    inp_ref[...] = inp

    gv = gv_ref[...]
    vn = gv * _rms_scale(gv) * gn_ref[...]
    mixed = coef_ref[0] * vn + gbias_ref[...]
    for d in range(1, t_new):
        mixed = mixed + coef_ref[d] * pltpu.roll(vn, d, axis=0)
    gm = gu_ref[...] * mixed

    _merge_tail(conv, gm, attn_ref, (gl0_ref, gl1_ref, gl2_ref), gb_ref, wb_ref, wo_ref, x_ref, y_ref)


def _slab_specs(tm):
    col = lambda c: pl.BlockSpec((tm, 512), lambda i, c=c: (i, c))
    gl = lambda g: pl.BlockSpec((tm, D_MODEL), lambda i, g=g: (i, COL_GL // 2 + g))
    return [col(COL_CB), col(COL_CC), col(COL_CH), col(COL_GU), col(COL_GV), gl(0), gl(1), gl(2)]


def _full(shape):
    return pl.BlockSpec(shape, lambda i: (0,) * len(shape))


def _merge_prompt(slab, attn, x, cw, gn, ws_cat, gbias, gb, wb, wo, seq, tm):
    m = slab.shape[0]
    ntiles = m // tm
    halo = lambda c: pl.BlockSpec((8, 512), lambda i, c=c: (jnp.maximum(i * (tm // 8) - 1, 0), c))
    y, tail = pl.pallas_call(
        functools.partial(_merge_prompt_kernel, tiles_per_seq=seq // tm),
        out_shape=(jax.ShapeDtypeStruct((m, D_MODEL), F32), jax.ShapeDtypeStruct((ntiles, 8, CONV_W), F32)),
        grid=(ntiles,),
        in_specs=_slab_specs(tm) + [halo(COL_CC), halo(COL_CH),
                                    pl.BlockSpec((tm, ATTN_W), lambda i: (i, 0)),
                                    pl.BlockSpec((tm, D_MODEL), lambda i: (i, 0)),
                                    _full(cw.shape), _full(gn.shape), _full(ws_cat.shape), _full(gbias.shape),
                                    _full(gb.shape), _full(wb.shape), _full(wo.shape)],
        out_specs=(pl.BlockSpec((tm, D_MODEL), lambda i: (i, 0)),
                   pl.BlockSpec((None, 8, CONV_W), lambda i: (i, 0, 0))),
        compiler_params=pltpu.CompilerParams(
            dimension_semantics=("parallel",), vmem_limit_bytes=V7X_VMEM_LIMIT),
        name="merge_prompt",
    )(*([slab] * 10), attn, x, cw, gn, ws_cat, gbias, gb, wb, wo)
    return y, tail


def _merge_sample(slab, prev1, prev2, attn, x, cw, gn, coef, gbias_rows, gb, wb, wo, t_new, tm):
    m = slab.shape[0]
    rows = lambda w: pl.BlockSpec((tm, w), lambda i: (i, 0))
    y, inp = pl.pallas_call(
        functools.partial(_merge_sample_kernel, t_new=t_new),
        out_shape=(jax.ShapeDtypeStruct((m, D_MODEL), F32), jax.ShapeDtypeStruct((m, CONV_W), F32)),
        grid=(m // tm,),
        in_specs=_slab_specs(tm) + [rows(CONV_W), rows(CONV_W), rows(ATTN_W), rows(D_MODEL),
                                    _full(cw.shape), _full(gn.shape),
                                    pl.BlockSpec((t_new, tm, GMLP_W), lambda i: (0, i, 0)), rows(GMLP_W),
                                    _full(gb.shape), _full(wb.shape), _full(wo.shape)],
        out_specs=(rows(D_MODEL), rows(CONV_W)),
        compiler_params=pltpu.CompilerParams(
            dimension_semantics=("parallel",), vmem_limit_bytes=V7X_VMEM_LIMIT),
        name="merge_sample",
    )(*([slab] * 8), prev1, prev2, attn, x, cw, gn, coef, gbias_rows, gb, wb, wo)
    return y, inp


def _mlp_kernel(x_ref, g_ref, wu_ref, wd_ref, o_ref, xn_ref, acc_ref):
    f = pl.program_id(1)

    @pl.when(f == 0)
    def _():
        x = x_ref[...]
        xn_ref[...] = (x * _rms_scale(x) * g_ref[...]).astype(BF16)
        acc_ref[...] = jnp.zeros_like(acc_ref)

    h = jnp.maximum(jnp.dot(xn_ref[...], wu_ref[...], preferred_element_type=F32), 0.0)
    acc_ref[...] += jnp.dot((h * h).astype(BF16), wd_ref[...], preferred_element_type=F32)

    @pl.when(f == pl.num_programs(1) - 1)
    def _():
        o_ref[...] = x_ref[...] + acc_ref[...]


def _mlp(x, g, wu, wd, tm):
    m = x.shape[0]
    tf = 1024
    return pl.pallas_call(
        _mlp_kernel,
        out_shape=jax.ShapeDtypeStruct((m, D_MODEL), F32),
        grid=(m // tm, D_FF // tf),
        in_specs=[
            pl.BlockSpec((tm, D_MODEL), lambda i, f: (i, 0)),
            pl.BlockSpec((1, D_MODEL), lambda i, f: (0, 0)),
            pl.BlockSpec((D_MODEL, tf), lambda i, f: (0, f)),
            pl.BlockSpec((tf, D_MODEL), lambda i, f: (f, 0)),
        ],
        out_specs=pl.BlockSpec((tm, D_MODEL), lambda i, f: (i, 0)),
        scratch_shapes=[pltpu.VMEM((tm, D_MODEL), BF16), pltpu.VMEM((tm, D_MODEL), F32)],
        compiler_params=pltpu.CompilerParams(
            dimension_semantics=("parallel", "arbitrary"), vmem_limit_bytes=V7X_VMEM_LIMIT),
        name="mlp",
    )(x, g, wu, wd)


def _head_mean_matrix():
    idx = np.arange(512) // HEAD_DIM
    return jnp.asarray((idx[:, None] == idx[None, :]).astype(np.float32) / HEAD_DIM, dtype=BF16)


def kernel(x_prompt, x_sample, cache_k, cache_v, state_conv, page_table, norm1_g, w_in, q_norm_g, k_norm_g,
           conv_w, gmlp_norm_g, gmlp_ws, gmlp_b, gate_b, w_branch, w_out, norm2_g, w_up, w_down):
    batch, seq, _ = x_prompt.shape
    nseq, t_new, _ = x_sample.shape
    depth = w_in.shape[0]
    n_pages = page_table.shape[1]
    gdim = GMLP_W // GMLP_GROUPS
    assert t_new == 4 and seq % MOBA_BLOCK == 0 and (n_pages * PAGE_SIZE) % MOBA_BLOCK == 0

    hp = x_prompt.reshape(batch * seq, D_MODEL)
    hs = x_sample.reshape(nseq * t_new, D_MODEL)
    ms = hs.shape[0]
    pt_flat = page_table.reshape(-1).astype(jnp.int32)
    n_phys = cache_k.shape[1]
    cache_kt = cache_k.transpose(0, 1, 3, 4, 2).reshape(depth, n_phys, ATTN_W, PAGE_SIZE)
    cache_vt = cache_v.transpose(0, 1, 3, 4, 2).reshape(depth, n_phys, ATTN_W, PAGE_SIZE)
    gmat = _head_mean_matrix()

    kp_l, vp_l, ks_l, vs_l, cp_l, cs_l, gv_l = [], [], [], [], [], [], []
    for l in range(depth):
        w_in_b = w_in[l].astype(BF16)
        wb_b = w_branch[l].astype(BF16)
        wo_b = w_out[l].astype(BF16)
        wu_b = w_up[l].astype(BF16)
        wd_b = w_down[l].astype(BF16)
        g1 = norm1_g[l].reshape(1, D_MODEL)
        g2 = norm2_g[l].reshape(1, D_MODEL)
        qkg = jnp.stack([jnp.tile(q_norm_g[l], N_HEADS), jnp.tile(k_norm_g[l], N_HEADS)]).reshape(2, 1, ATTN_W)
        gn = gmlp_norm_g[l].reshape(1, GMLP_W)
        ws = gmlp_ws[l]
        ws_cat = ws.reshape(GMLP_GROUPS // 2, 2, CHUNK, CHUNK).transpose(0, 2, 1, 3).reshape(
            GMLP_GROUPS // 2, CHUNK, 2 * CHUNK)
        gbias = jnp.repeat(gmlp_b[l].T, gdim, axis=1)

        slab = _in_proj(hp, g1, w_in_b, qkg, gmat, tm=2048)
        attn, k_t, v_t = _attn_prompt(slab, batch, seq)
        tm_p = 256
        hp, tail = _merge_prompt(slab, attn, hp, conv_w[l], gn, ws_cat, gbias, gate_b[l], wb_b, wo_b, seq, tm_p)
        hp = _mlp(hp, g2, wu_b, wd_b, tm=1024)
        kp_l.append(k_t.reshape(batch, N_HEADS, HEAD_DIM, seq).transpose(0, 3, 1, 2))
        vp_l.append(v_t.reshape(batch, N_HEADS, HEAD_DIM, seq).transpose(0, 3, 1, 2))
        cp_l.append(tail.reshape(batch, seq // tm_p, 8, CONV_W)[:, -1, 6:8, :])

        slab_s = _in_proj(hs, g1, w_in_b, qkg, gmat, tm=ms)
        q3 = slab_s[:, COL_Q * 512:(COL_Q + 1) * 512].reshape(nseq, t_new, ATTN_W)
        k3 = slab_s[:, COL_K * 512:(COL_K + 1) * 512].reshape(nseq, t_new, ATTN_W)
        v3 = slab_s[:, COL_V * 512:(COL_V + 1) * 512].reshape(nseq, t_new, ATTN_W)
        attn_s = _attn_sample(l, pt_flat, q3, k3, v3, cache_kt, cache_vt, n_pages).reshape(ms, ATTN_W)
        st = state_conv[l]
        zero = jnp.zeros((nseq, 1, CONV_W), F32)
        prev1 = jnp.concatenate([st[:, 1:2], zero, zero, zero], axis=1).reshape(ms, CONV_W)
        prev2 = jnp.concatenate([st[:, 0:1], st[:, 1:2], zero, zero], axis=1).reshape(ms, CONV_W)
        tt = np.arange(t_new)
        coef_rows = []
        for d in range(t_new):
            wd_sel = ws[:, tt, np.maximum(tt - d, 0)] * jnp.asarray((tt >= d).astype(np.float32))
            coef_rows.append(jnp.repeat(wd_sel.T, gdim, axis=1))
        coef = jnp.tile(jnp.stack(coef_rows), (1, nseq, 1))
        gbias_rows = jnp.tile(gbias[:t_new], (nseq, 1))
        hs, inp_s = _merge_sample(slab_s, prev1, prev2, attn_s, hs, conv_w[l], gn, coef, gbias_rows, gate_b[l],
                                  wb_b, wo_b, t_new, tm=ms)
        hs = _mlp(hs, g2, wu_b, wd_b, tm=ms)
        ks_l.append(k3.reshape(nseq, t_new, N_HEADS, HEAD_DIM))
        vs_l.append(v3.reshape(nseq, t_new, N_HEADS, HEAD_DIM))
        cs_l.append(inp_s.reshape(nseq, t_new, CONV_W)[:, t_new - 2:, :])
        gv_l.append(slab_s[:, COL_GV * 512:(COL_GV + 1) * 512].reshape(nseq, t_new, GMLP_W))

    return (hp.reshape(batch, seq, D_MODEL), hs.reshape(nseq, t_new, D_MODEL),
            jnp.stack(kp_l), jnp.stack(vp_l), jnp.stack(ks_l), jnp.stack(vs_l),
            jnp.stack(cp_l), jnp.stack(cs_l), jnp.stack(gv_l))
```

```python
import functools

import numpy as np
import jax
import jax.numpy as jnp
from jax import lax
from jax.experimental import pallas as pl
from jax.experimental.pallas import tpu as pltpu

D_MODEL = 1024
N_HEADS = 8
HEAD_DIM = 64
ATTN_W = N_HEADS * HEAD_DIM
CONV_W = 512
GMLP_W = 512
GMLP_GROUPS = 8
CHUNK = 128
N_BRANCH = 3
MOBA_BLOCK = 256
MOBA_TOPK = 3
PAGE_SIZE = 128
D_FF = 4 * D_MODEL
EPS = 1e-6
NEG = -1e30
IN_COLS = 3 * ATTN_W + 3 * CONV_W + 2 * GMLP_W + N_BRANCH * D_MODEL
COL_Q, COL_K, COL_V, COL_CB, COL_CC, COL_CH, COL_GU, COL_GV, COL_GL = 0, 1, 2, 3, 4, 5, 6, 7, 8

LANES = 128
SUBLANES = 8
V7X_VMEM_LIMIT = 56 * 1024 * 1024

BF16 = jnp.bfloat16
F32 = jnp.float32
NT_DIMS = (((1,), (1,)), ((), ()))


def _rms_scale(x):
    return lax.rsqrt(jnp.mean(x * x, axis=-1, keepdims=True) + EPS)


def _inproj_kernel(x_ref, g_ref, w_ref, qkg_ref, gmat_ref, o_ref, xn_ref):
    j = pl.program_id(1)

    @pl.when(j == 0)
    def _():
        x = x_ref[...]
        xn_ref[...] = (x * _rms_scale(x) * g_ref[...]).astype(BF16)

    acc = jnp.dot(xn_ref[...], w_ref[...], preferred_element_type=F32)

    @pl.when(j <= COL_K)
    def _():
        sq = acc * acc
        hi = sq.astype(BF16)
        lo = (sq - hi.astype(F32)).astype(BF16)
        gmat = gmat_ref[...]
        gw = gmat.shape[0]
        ms = jnp.concatenate(
            [jnp.dot(hi[:, c:c + gw], gmat, preferred_element_type=F32)
             + jnp.dot(lo[:, c:c + gw], gmat, preferred_element_type=F32) for c in range(0, acc.shape[1], gw)],
            axis=1)
        o_ref[...] = acc * lax.rsqrt(ms + EPS) * qkg_ref[...]

    @pl.when(j > COL_K)
    def _():
        o_ref[...] = acc


def _in_proj(x, g, w_bf, qkg, gmat, tm):
    m = x.shape[0]
    tn = 512
    return pl.pallas_call(
        _inproj_kernel,
        out_shape=jax.ShapeDtypeStruct((m, w_bf.shape[1]), F32),
        grid=(m // tm, w_bf.shape[1] // tn),
        in_specs=[
            pl.BlockSpec((tm, D_MODEL), lambda i, j: (i, 0)),
            pl.BlockSpec((1, D_MODEL), lambda i, j: (0, 0)),
            pl.BlockSpec((D_MODEL, tn), lambda i, j: (0, j)),
            pl.BlockSpec((None, 1, tn), lambda i, j: (jnp.minimum(j, COL_K), 0, 0)),
            pl.BlockSpec(gmat.shape, lambda i, j: (0, 0)),
        ],
        out_specs=pl.BlockSpec((tm, tn), lambda i, j: (i, j)),
        scratch_shapes=[pltpu.VMEM((tm, D_MODEL), BF16)],
        compiler_params=pltpu.CompilerParams(
            dimension_semantics=("parallel", "arbitrary"), vmem_limit_bytes=V7X_VMEM_LIMIT),
        name="in_proj",
    )(x, g, w_bf, qkg, gmat)


def _attn_prompt_kernel(q_ref, k_ref, v_ref, o_ref, kto_ref, vto_ref, kb_ref, vt_ref, km_ref, sel_ref, qs_ref, m_ref, acc_ref,
                        s0_ref, s1_ref, *, group):
    seq = k_ref.shape[0]
    nblk = seq // MOBA_BLOCK
    lane = lax.broadcasted_iota(jnp.int32, (1, LANES), 1)
    lo_lanes = lane < HEAD_DIM
    lo_rows = lax.broadcasted_iota(jnp.int32, (LANES, 1), 0) < HEAD_DIM

    def _init():
        kf = k_ref[...]
        kb_ref[...] = kf.astype(BF16)
        km_ref[...] = jnp.sum(kf.reshape(nblk, MOBA_BLOCK, LANES), axis=1) * (1.0 / MOBA_BLOCK)
        for n in range(nblk):
            v_t = v_ref[n * MOBA_BLOCK:(n + 1) * MOBA_BLOCK, :].T
            kto_ref[:, n * MOBA_BLOCK:(n + 1) * MOBA_BLOCK] = kf[n * MOBA_BLOCK:(n + 1) * MOBA_BLOCK, :].T
            vto_ref[:, n * MOBA_BLOCK:(n + 1) * MOBA_BLOCK] = v_t
            vt_ref[0, n] = jnp.where(lo_rows, v_t, 1.0).astype(BF16)
            vt_ref[1, n] = jnp.where(lo_rows, 1.0, v_t).astype(BF16)

    _init()

    def tile_body(t, carry):
        koff = pl.multiple_of(t * MOBA_BLOCK, MOBA_BLOCK)
        q = q_ref[pl.ds(koff, MOBA_BLOCK), :]
        km = km_ref[...]
        blk = lax.broadcasted_iota(jnp.int32, (nblk, 1), 0)
        krow = lax.broadcasted_iota(jnp.int32, (MOBA_BLOCK, MOBA_BLOCK), 0)
        qcol = lax.broadcasted_iota(jnp.int32, (MOBA_BLOCK, MOBA_BLOCK), 1)
        s_refs = (s0_ref, s1_ref)

        def qk_stage(gi, s_ref):
            gi = jnp.minimum(gi, nblk // group - 1)
            for h in range(2):
                qs = qs_ref[h]
                for u in range(group):
                    off = pl.multiple_of((gi * group + u) * MOBA_BLOCK, MOBA_BLOCK)
                    s_ref[h, u] = lax.dot_general(kb_ref[pl.ds(off, MOBA_BLOCK), :], qs, NT_DIMS,
                                                  preferred_element_type=F32)

        def softmax_pv_stage(gi, s_ref):
            for h in range(2):
                m_old = m_ref[h]
                m_new = m_old
                picks = []
                for u in range(group):
                    picked = sel_ref[h, pl.ds(gi * group + u, 1), :] > 0.5
                    blk_max = jnp.max(s_ref[h, u], axis=0, keepdims=True)
                    m_new = jnp.where(picked, jnp.maximum(m_new, blk_max), m_new)
                    picks.append(picked)
                acc = jnp.exp(m_old - m_new) * acc_ref[h]
                for u in range(group):
                    p = jnp.exp(s_ref[h, u] - jnp.where(picks[u], m_new, -NEG)).astype(BF16)
                    acc = acc + jnp.dot(vt_ref[h, gi * group + u], p, preferred_element_type=F32)
                m_ref[h] = m_new
                acc_ref[h] = acc

        gates = []
        s_own = []
        for h in range(2):
            qh = jnp.where(lo_lanes if h == 0 else jnp.logical_not(lo_lanes), q, 0.0)
            gates.append(lax.dot_general(km, qh, NT_DIMS, precision=lax.Precision.HIGHEST,
                                         preferred_element_type=F32))
            qs = (qh * (HEAD_DIM ** -0.5)).astype(BF16)
            qs_ref[h] = qs
            s_own.append(lax.dot_general(kb_ref[pl.ds(koff, MOBA_BLOCK), :], qs, NT_DIMS,
                                         preferred_element_type=F32))
        qk_stage(0, s_refs[0])

        for h in range(2):
            g = jnp.where(blk < t, gates[h], -jnp.inf)
            rank = jnp.zeros_like(g)
            for mm in range(nblk):
                gm = g[mm:mm + 1, :]
                beats = jnp.logical_or(gm > g, jnp.logical_and(gm == g, blk > mm))
                rank = rank + jnp.where(beats, 1.0, 0.0)
            sel_ref[h] = jnp.where(jnp.logical_and(rank < float(MOBA_TOPK), blk < t), 1.0, 0.0)

            s = jnp.where(krow <= qcol, s_own[h], NEG)
            m0 = jnp.max(s, axis=0, keepdims=True)
            p = jnp.exp(s - m0).astype(BF16)
            m_ref[h] = m0
            acc_ref[h] = jnp.dot(vt_ref[h, t], p, preferred_element_type=F32)

        ngroups = (t + group - 1) // group

        def body(j, inner):
            g0 = 2 * j
            qk_stage(g0 + 1, s_refs[1])
            softmax_pv_stage(g0, s_refs[0])
            qk_stage(g0 + 2, s_refs[0])
            softmax_pv_stage(g0 + 1, s_refs[1])
            return inner

        lax.fori_loop(0, (ngroups + 1) // 2, body, 0)

        a0 = acc_ref[0]
        a1 = acc_ref[1]
        o_t = jnp.where(lo_rows, a0 * (1.0 / a0[HEAD_DIM:HEAD_DIM + 1, :]), a1 * (1.0 / a1[0:1, :]))
        o_ref[pl.ds(koff, MOBA_BLOCK), :] = o_t.T.astype(o_ref.dtype)
        return carry

    lax.fori_loop(0, nblk, tile_body, 0)


def _attn_prompt(slab, batch, seq, group=2):
    m = slab.shape[0]
    nq = seq // MOBA_BLOCK
    npair = ATTN_W // LANES
    cpb = 512 // LANES
    assert nq % group == 0
    return pl.pallas_call(
        functools.partial(_attn_prompt_kernel, group=group),
        out_shape=(jax.ShapeDtypeStruct((m, ATTN_W), BF16),
                   jax.ShapeDtypeStruct((batch, ATTN_W, seq), F32),
                   jax.ShapeDtypeStruct((batch, ATTN_W, seq), F32)),
        grid=(batch, npair),
        in_specs=[
            pl.BlockSpec((seq, LANES), lambda b, p: (b, COL_Q * cpb + p)),
            pl.BlockSpec((seq, LANES), lambda b, p: (b, COL_K * cpb + p)),
            pl.BlockSpec((seq, LANES), lambda b, p: (b, COL_V * cpb + p)),
        ],
        out_specs=(pl.BlockSpec((seq, LANES), lambda b, p: (b, p)),
                   pl.BlockSpec((None, LANES, seq), lambda b, p: (b, p, 0)),
                   pl.BlockSpec((None, LANES, seq), lambda b, p: (b, p, 0))),
        scratch_shapes=[
            pltpu.VMEM((seq, LANES), BF16),
            pltpu.VMEM((2, nq, LANES, MOBA_BLOCK), BF16),
            pltpu.VMEM((nq, LANES), F32),
            pltpu.VMEM((2, nq, MOBA_BLOCK), F32),
            pltpu.VMEM((2, MOBA_BLOCK, LANES), BF16),
            pltpu.VMEM((2, 1, MOBA_BLOCK), F32),
            pltpu.VMEM((2, LANES, MOBA_BLOCK), F32),
            pltpu.VMEM((2, group, MOBA_BLOCK, MOBA_BLOCK), F32),
            pltpu.VMEM((2, group, MOBA_BLOCK, MOBA_BLOCK), F32),
        ],
        compiler_params=pltpu.CompilerParams(
            dimension_semantics=("parallel", "parallel"), vmem_limit_bytes=V7X_VMEM_LIMIT),
        name="attn_prompt",
    )(slab, slab, slab)


def _attn_sample_kernel(pt_ref, q_ref, kn_ref, vn_ref, *refs, n_pages):
    del pt_ref
    kp_refs = refs[:n_pages]
    vp_refs = refs[n_pages:2 * n_pages]
    o_ref = refs[2 * n_pages]
    t_new = q_ref.shape[0]
    rows = t_new * N_HEADS
    pages_per_blk = MOBA_BLOCK // PAGE_SIZE
    n_past = n_pages // pages_per_blk
    assert n_past <= LANES

    q = q_ref[...]
    kn = kn_ref[...]
    vn = vn_ref[...]
    head_of_lane = lax.shift_right_logical(lax.broadcasted_iota(jnp.int32, (N_HEADS, ATTN_W), 1), 6)
    hm = head_of_lane == lax.broadcasted_iota(jnp.int32, (N_HEADS, ATTN_W), 0)
    qf = jnp.concatenate(
        [jnp.where(hm, jnp.broadcast_to(q[t:t + 1, :], (N_HEADS, ATTN_W)), 0.0) for t in range(t_new)], axis=0)
    hm_rows = jnp.concatenate([hm] * t_new, axis=0)
    trow = lax.shift_right_logical(lax.broadcasted_iota(jnp.int32, (rows, 1), 0), 3)
    lane = lax.broadcasted_iota(jnp.int32, (1, LANES), 1)

    kmean = jnp.zeros((ATTN_W, LANES), F32)
    for n in range(n_past):
        ksum = kp_refs[pages_per_blk * n][...]
        for j in range(1, pages_per_blk):
            ksum = ksum + kp_refs[pages_per_blk * n + j][...]
        km = jnp.sum(ksum, axis=1, keepdims=True) * (1.0 / MOBA_BLOCK)
        kmean = jnp.where(lane == n, km, kmean)
    gate = jnp.dot(qf, kmean, precision=lax.Precision.HIGHEST, preferred_element_type=F32)
    g = jnp.where(lane < n_past, gate, -jnp.inf)
    rank = jnp.zeros_like(g)
    for mm in range(n_past):
        gm = jnp.broadcast_to(g[:, mm:mm + 1], g.shape)
        beats = jnp.logical_or(gm > g, jnp.logical_and(gm == g, lane > mm))
        rank = rank + jnp.where(beats, 1.0, 0.0)
    sel = jnp.where(rank < float(min(MOBA_TOPK, n_past)), 1.0, 0.0)

    qs = qf * (HEAD_DIM ** -0.5)
    qb = qs.astype(BF16)
    s_past = []
    for p in range(n_pages):
        n = p // pages_per_blk
        s = jnp.dot(qb, kp_refs[p][...].astype(BF16), preferred_element_type=F32)
        s_past.append(jnp.where(sel[:, n:n + 1] > 0.5, s, NEG))
    s_own = []
    for i in range(t_new):
        s = jnp.sum(qs * kn[i:i + 1, :], axis=1, keepdims=True)
        s_own.append(jnp.where(trow >= i, s, NEG))

    m = s_own[0]
    for s in s_own[1:]:
        m = jnp.maximum(m, s)
    for s in s_past:
        m = jnp.maximum(m, jnp.max(s, axis=1, keepdims=True))

    acc = jnp.zeros((rows, ATTN_W), F32)
    den = jnp.zeros((rows, 1), F32)
    for p in range(n_pages):
        e = jnp.exp(s_past[p] - m)
        den = den + jnp.sum(e, axis=1, keepdims=True)
        acc = acc + lax.dot_general(e.astype(BF16), vp_refs[p][...].astype(BF16), NT_DIMS,
                                    preferred_element_type=F32)
    for i in range(t_new):
        e = jnp.exp(s_own[i] - m)
        den = den + e
        acc = acc + e * vn[i:i + 1, :]
    o = jnp.where(hm_rows, acc / den, 0.0)
    o_ref[...] = jnp.sum(o.reshape(t_new, N_HEADS, ATTN_W), axis=1)


def _attn_sample(layer, pt_flat, q3, k3, v3, cache_kt, cache_vt, n_pages):
    nseq, t_new, _ = q3.shape

    def page_spec(p):
        return pl.BlockSpec((None, None, ATTN_W, PAGE_SIZE),
                            lambda s, pt: (layer, pt[s * n_pages + p], 0, 0))

    new_spec = pl.BlockSpec((None, t_new, ATTN_W), lambda s, pt: (s, 0, 0))
    grid_spec = pltpu.PrefetchScalarGridSpec(
        num_scalar_prefetch=1,
        grid=(nseq,),
        in_specs=[new_spec, new_spec, new_spec]
        + [page_spec(p) for p in range(n_pages)] + [page_spec(p) for p in range(n_pages)],
        out_specs=new_spec,
    )
    return pl.pallas_call(
        functools.partial(_attn_sample_kernel, n_pages=n_pages),
        out_shape=jax.ShapeDtypeStruct((nseq, t_new, ATTN_W), F32),
        grid_spec=grid_spec,
        compiler_params=pltpu.CompilerParams(
            dimension_semantics=("arbitrary",), vmem_limit_bytes=V7X_VMEM_LIMIT),
        name="attn_sample",
    )(pt_flat, q3, k3, v3, *([cache_kt] * n_pages), *([cache_vt] * n_pages))


def _merge_tail(conv, gm, attn_ref, gate_lin, gb_ref, wb_ref, wo_ref, x_ref, y_ref):
    branches = (attn_ref[...].astype(BF16), conv.astype(BF16), gm.astype(BF16))
    merged = None
    for i in range(N_BRANCH):
        gate = jax.nn.sigmoid(gate_lin(i) + gb_ref[i:i + 1, :])
        term = gate * jnp.dot(branches[i], wb_ref[i], preferred_element_type=F32)
        merged = term if merged is None else merged + term
    y_ref[...] = x_ref[...] + jnp.dot(merged.astype(BF16), wo_ref[...], preferred_element_type=F32)


def _conv_taps(inp, m1, m2, cw_ref):
    return m2 * cw_ref[0:1, :] + m1 * cw_ref[1:2, :] + inp * cw_ref[2:3, :]


def _mixer_prompt_kernel(x_ref, attn_ref, g1_ref, w_ref, cw_ref, gn_ref, ws_ref, gbias_ref, gb_ref, wb_ref, wo_ref,
                         y_ref, tail_ref, hist_ref, *, tiles_per_seq):
    i = pl.program_id(0)
    tm = x_ref.shape[0]
    x = x_ref[...]
    xn = (x * _rms_scale(x) * g1_ref[...]).astype(BF16)

    def proj(c0, c1):
        return jnp.dot(xn, w_ref[:, c0 * 512:c1 * 512], preferred_element_type=F32)

    @pl.when(i == 0)
    def _():
        hist_ref[...] = jnp.zeros_like(hist_ref)

    cch = proj(COL_CC - COL_CB, COL_CH - COL_CB + 1)
    inp = cch[:, :CONV_W] * cch[:, CONV_W:]
    prev = jnp.where(i % tiles_per_seq == 0, 0.0, hist_ref[...])
    p0 = prev[6:7, :]
    p1 = prev[7:8, :]
    rowi = lax.broadcasted_iota(jnp.int32, (tm, 1), 0)
    m1 = jnp.where(rowi == 0, p1, pltpu.roll(inp, 1, axis=0))
    m2 = jnp.where(rowi == 0, p0, jnp.where(rowi == 1, p1, pltpu.roll(inp, 2, axis=0)))
    conv = proj(0, 1) * _conv_taps(inp, m1, m2, cw_ref)
    tail_ref[...] = inp[tm - 8:tm, :]
    hist_ref[...] = inp[tm - 8:tm, :]

    guv = proj(COL_GU - COL_CB, COL_GV - COL_CB + 1)
    gu = guv[:, :GMLP_W]
    gv = guv[:, GMLP_W:]
    vn = gv * _rms_scale(gv) * gn_ref[...]
    lane = lax.broadcasted_iota(jnp.int32, (1, LANES), 1)
    lo_half = lane < (GMLP_W // GMLP_GROUPS)
    wrow = lax.broadcasted_iota(jnp.int32, (CHUNK, 2 * CHUNK), 0)
    wcol = lax.broadcasted_iota(jnp.int32, (CHUNK, 2 * CHUNK), 1) & (CHUNK - 1)
    cols = []
    for c in range(GMLP_W // LANES):
        w = jnp.where(wcol <= wrow, ws_ref[c], 0.0).astype(BF16)
        vc = vn[:, c * LANES:(c + 1) * LANES]
        chunks = []
        for k in range(tm // CHUNK):
            vck = vc[k * CHUNK:(k + 1) * CHUNK, :]
            rhs = jnp.concatenate([jnp.where(lo_half, vck, 0.0), jnp.where(lo_half, 0.0, vck)], axis=0)
            mixed = jnp.dot(w, rhs.astype(BF16), preferred_element_type=F32)
            chunks.append(mixed + gbias_ref[:, c * LANES:(c + 1) * LANES])
        cols.append(jnp.concatenate(chunks, axis=0))
    gm = gu * jnp.concatenate(cols, axis=1)

    gl0 = COL_GL - COL_CB
    _merge_tail(conv, gm, attn_ref, lambda b: proj(gl0 + 2 * b, gl0 + 2 * b + 2), gb_ref, wb_ref, wo_ref,
                x_ref, y_ref)


def _merge_sample_kernel(cb_ref, cc_ref, ch_ref, gu_ref, gv_ref, gl0_ref, gl1_ref, gl2_ref, prev1_ref, prev2_ref,
                         attn_ref, x_ref, cw_ref, gn_ref, coef_ref, gbias_ref, gb_ref, wb_ref, wo_ref,
                         y_ref, inp_ref, *, t_new):
    tm = cb_ref.shape[0]
    inp = cc_ref[...] * ch_ref[...]
    tpos = lax.broadcasted_iota(jnp.int32, (tm, 1), 0) & (t_new - 1)
    m1 = jnp.where(tpos == 0, prev1_ref[...], pltpu.roll(inp, 1, axis=0))
    m2 = jnp.where(tpos < 2, prev2_ref[...], pltpu.roll(inp, 2, axis=0))
    conv = cb_ref[...] * _conv_taps(inp, m1, m2, cw_ref)
    inp_ref[...] = inp

    gv = gv_ref[...]
    vn = gv * _rms_scale(gv) * gn_ref[...]
    mixed = coef_ref[0] * vn + gbias_ref[...]
    for d in range(1, t_new):
        mixed = mixed + coef_ref[d] * pltpu.roll(vn, d, axis=0)
    gm = gu_ref[...] * mixed

    gl_refs = (gl0_ref, gl1_ref, gl2_ref)
    _merge_tail(conv, gm, attn_ref, lambda b: gl_refs[b][...], gb_ref, wb_ref, wo_ref, x_ref, y_ref)


def _slab_specs(tm):
    col = lambda c: pl.BlockSpec((tm, 512), lambda i, c=c: (i, c))
    gl = lambda g: pl.BlockSpec((tm, D_MODEL), lambda i, g=g: (i, COL_GL // 2 + g))
    return [col(COL_CB), col(COL_CC), col(COL_CH), col(COL_GU), col(COL_GV), gl(0), gl(1), gl(2)]


def _full(shape):
    return pl.BlockSpec(shape, lambda i: (0,) * len(shape))


def _resident(shape):
    return pl.BlockSpec(shape, lambda i: (0,) * len(shape), pipeline_mode=pl.Buffered(1))


def _mixer_prompt(x, attn, g1, w_rest, cw, gn, ws_cat, gbias, gb, wb, wo, seq, tm):
    m = x.shape[0]
    ntiles = m // tm
    y, tail = pl.pallas_call(
        functools.partial(_mixer_prompt_kernel, tiles_per_seq=seq // tm),
        out_shape=(jax.ShapeDtypeStruct((m, D_MODEL), F32), jax.ShapeDtypeStruct((ntiles, 8, CONV_W), F32)),
        grid=(ntiles,),
        in_specs=[pl.BlockSpec((tm, D_MODEL), lambda i: (i, 0)),
                  pl.BlockSpec((tm, ATTN_W), lambda i: (i, 0)),
                  _full(g1.shape), _resident(w_rest.shape), _full(cw.shape), _full(gn.shape),
                  _full(ws_cat.shape), _full(gbias.shape), _full(gb.shape),
                  _resident(wb.shape), _resident(wo.shape)],
        out_specs=(pl.BlockSpec((tm, D_MODEL), lambda i: (i, 0)),
                   pl.BlockSpec((None, 8, CONV_W), lambda i: (i, 0, 0))),
        scratch_shapes=[pltpu.VMEM((8, CONV_W), F32)],
        compiler_params=pltpu.CompilerParams(
            dimension_semantics=("arbitrary",), vmem_limit_bytes=V7X_VMEM_LIMIT),
        name="mixer_prompt",
    )(x, attn, g1, w_rest, cw, gn, ws_cat, gbias, gb, wb, wo)
    return y, tail


def _merge_sample(slab, prev1, prev2, attn, x, cw, gn, coef, gbias_rows, gb, wb, wo, t_new, tm):
    m = slab.shape[0]
    rows = lambda w: pl.BlockSpec((tm, w), lambda i: (i, 0))
    y, inp = pl.pallas_call(
        functools.partial(_merge_sample_kernel, t_new=t_new),
        out_shape=(jax.ShapeDtypeStruct((m, D_MODEL), F32), jax.ShapeDtypeStruct((m, CONV_W), F32)),
        grid=(m // tm,),
        in_specs=_slab_specs(tm) + [rows(CONV_W), rows(CONV_W), rows(ATTN_W), rows(D_MODEL),
                                    _full(cw.shape), _full(gn.shape),
                                    pl.BlockSpec((t_new, tm, GMLP_W), lambda i: (0, i, 0)), rows(GMLP_W),
                                    _full(gb.shape), _full(wb.shape), _full(wo.shape)],
        out_specs=(rows(D_MODEL), rows(CONV_W)),
        compiler_params=pltpu.CompilerParams(
            dimension_semantics=("parallel",), vmem_limit_bytes=V7X_VMEM_LIMIT),
        name="merge_sample",
    )(*([slab] * 8), prev1, prev2, attn, x, cw, gn, coef, gbias_rows, gb, wb, wo)
    return y, inp


def _mlp_kernel(x_ref, g_ref, wu_ref, wd_ref, o_ref, xn_ref, acc_ref):
    f = pl.program_id(1)

    @pl.when(f == 0)
    def _():
        x = x_ref[...]
        xn_ref[...] = (x * _rms_scale(x) * g_ref[...]).astype(BF16)
        acc_ref[...] = jnp.zeros_like(acc_ref)

    h = jnp.maximum(jnp.dot(xn_ref[...], wu_ref[...], preferred_element_type=F32), 0.0)
    acc_ref[...] += jnp.dot((h * h).astype(BF16), wd_ref[...], preferred_element_type=F32)

    @pl.when(f == pl.num_programs(1) - 1)
    def _():
        o_ref[...] = x_ref[...] + acc_ref[...]


def _mlp(x, g, wu, wd, tm):
    m = x.shape[0]
    tf = 1024
    return pl.pallas_call(
        _mlp_kernel,
        out_shape=jax.ShapeDtypeStruct((m, D_MODEL), F32),
        grid=(m // tm, D_FF // tf),
        in_specs=[
            pl.BlockSpec((tm, D_MODEL), lambda i, f: (i, 0)),
            pl.BlockSpec((1, D_MODEL), lambda i, f: (0, 0)),
            pl.BlockSpec((D_MODEL, tf), lambda i, f: (0, f)),
            pl.BlockSpec((tf, D_MODEL), lambda i, f: (f, 0)),
        ],
        out_specs=pl.BlockSpec((tm, D_MODEL), lambda i, f: (i, 0)),
        scratch_shapes=[pltpu.VMEM((tm, D_MODEL), BF16), pltpu.VMEM((tm, D_MODEL), F32)],
        compiler_params=pltpu.CompilerParams(
            dimension_semantics=("parallel", "arbitrary"), vmem_limit_bytes=V7X_VMEM_LIMIT),
        name="mlp",
    )(x, g, wu, wd)


MXU_TILE = 256


def _head_mean_matrix():
    idx = np.arange(MXU_TILE) // HEAD_DIM
    return jnp.asarray((idx[:, None] == idx[None, :]).astype(np.float32) / HEAD_DIM, dtype=BF16)


def kernel(x_prompt, x_sample, cache_k, cache_v, state_conv, page_table, norm1_g, w_in, q_norm_g, k_norm_g,
           conv_w, gmlp_norm_g, gmlp_ws, gmlp_b, gate_b, w_branch, w_out, norm2_g, w_up, w_down):
    batch, seq, _ = x_prompt.shape
    nseq, t_new, _ = x_sample.shape
    depth = w_in.shape[0]
    n_pages = page_table.shape[1]
    gdim = GMLP_W // GMLP_GROUPS
    assert t_new == 4 and seq % MOBA_BLOCK == 0 and (n_pages * PAGE_SIZE) % MOBA_BLOCK == 0

    hp = x_prompt.reshape(batch * seq, D_MODEL)
    hs = x_sample.reshape(nseq * t_new, D_MODEL)
    ms = hs.shape[0]
    pt_flat = page_table.reshape(-1).astype(jnp.int32)
    n_phys = cache_k.shape[1]
    cache_kt = cache_k.transpose(0, 1, 3, 4, 2).reshape(depth, n_phys, ATTN_W, PAGE_SIZE)
    cache_vt = cache_v.transpose(0, 1, 3, 4, 2).reshape(depth, n_phys, ATTN_W, PAGE_SIZE)
    gmat = _head_mean_matrix()

    kp_l, vp_l, ks_l, vs_l, cp_l, cs_l, gv_l = [], [], [], [], [], [], []
    for l in range(depth):
        w_in_b = w_in[l].astype(BF16)
        wb_b = w_branch[l].astype(BF16)
        wo_b = w_out[l].astype(BF16)
        wu_b = w_up[l].astype(BF16)
        wd_b = w_down[l].astype(BF16)
        g1 = norm1_g[l].reshape(1, D_MODEL)
        g2 = norm2_g[l].reshape(1, D_MODEL)
        qkg = jnp.stack([jnp.tile(q_norm_g[l], N_HEADS), jnp.tile(k_norm_g[l], N_HEADS)]).reshape(2, 1, ATTN_W)
        gn = gmlp_norm_g[l].reshape(1, GMLP_W)
        ws = gmlp_ws[l]
        ws_cat = ws.reshape(GMLP_GROUPS // 2, 2, CHUNK, CHUNK).transpose(0, 2, 1, 3).reshape(
            GMLP_GROUPS // 2, CHUNK, 2 * CHUNK)
        gbias = jnp.repeat(gmlp_b[l].T, gdim, axis=1)

        slab = _in_proj(hp, g1, w_in_b[:, :COL_CB * 512], qkg, gmat, tm=2048)
        attn, k_t, v_t = _attn_prompt(slab, batch, seq)
        tm_p = 512
        hp, tail = _mixer_prompt(hp, attn, g1, w_in_b[:, COL_CB * 512:], conv_w[l], gn, ws_cat, gbias, gate_b[l],
                                 wb_b, wo_b, seq, tm_p)
        hp = _mlp(hp, g2, wu_b, wd_b, tm=1024)
        kp_l.append(k_t.reshape(batch, N_HEADS, HEAD_DIM, seq).transpose(0, 3, 1, 2))
        vp_l.append(v_t.reshape(batch, N_HEADS, HEAD_DIM, seq).transpose(0, 3, 1, 2))
        cp_l.append(tail.reshape(batch, seq // tm_p, 8, CONV_W)[:, -1, 6:8, :])

        slab_s = _in_proj(hs, g1, w_in_b, qkg, gmat, tm=ms)
        q3 = slab_s[:, COL_Q * 512:(COL_Q + 1) * 512].reshape(nseq, t_new, ATTN_W)
        k3 = slab_s[:, COL_K * 512:(COL_K + 1) * 512].reshape(nseq, t_new, ATTN_W)
        v3 = slab_s[:, COL_V * 512:(COL_V + 1) * 512].reshape(nseq, t_new, ATTN_W)
        attn_s = _attn_sample(l, pt_flat, q3, k3, v3, cache_kt, cache_vt, n_pages).reshape(ms, ATTN_W)
        st = state_conv[l]
        zero = jnp.zeros((nseq, 1, CONV_W), F32)
        prev1 = jnp.concatenate([st[:, 1:2], zero, zero, zero], axis=1).reshape(ms, CONV_W)
        prev2 = jnp.concatenate([st[:, 0:1], st[:, 1:2], zero, zero], axis=1).reshape(ms, CONV_W)
        tt = np.arange(t_new)
        coef_rows = []
        for d in range(t_new):
            wd_sel = ws[:, tt, np.maximum(tt - d, 0)] * jnp.asarray((tt >= d).astype(np.float32))
            coef_rows.append(jnp.repeat(wd_sel.T, gdim, axis=1))
        coef = jnp.tile(jnp.stack(coef_rows), (1, nseq, 1))
        gbias_rows = jnp.tile(gbias[:t_new], (nseq, 1))
        hs, inp_s = _merge_sample(slab_s, prev1, prev2, attn_s, hs, conv_w[l], gn, coef, gbias_rows, gate_b[l],
                                  wb_b, wo_b, t_new, tm=ms)
        hs = _mlp(hs, g2, wu_b, wd_b, tm=ms)
        ks_l.append(k3.reshape(nseq, t_new, N_HEADS, HEAD_DIM))
        vs_l.append(v3.reshape(nseq, t_new, N_HEADS, HEAD_DIM))
        cs_l.append(inp_s.reshape(nseq, t_new, CONV_W)[:, t_new - 2:, :])
        gv_l.append(slab_s[:, COL_GV * 512:(COL_GV + 1) * 512].reshape(nseq, t_new, GMLP_W))

    return (hp.reshape(batch, seq, D_MODEL), hs.reshape(nseq, t_new, D_MODEL),
            jnp.stack(kp_l), jnp.stack(vp_l), jnp.stack(ks_l), jnp.stack(vs_l),
            jnp.stack(cp_l), jnp.stack(cs_l), jnp.stack(gv_l))
```

```python
import functools

import numpy as np
import jax
import jax.numpy as jnp
from jax import lax
from jax.experimental import pallas as pl
from jax.experimental.pallas import tpu as pltpu

D_MODEL = 1024
N_HEADS = 8
HEAD_DIM = 64
ATTN_W = N_HEADS * HEAD_DIM
CONV_W = 512
GMLP_W = 512
GMLP_GROUPS = 8
CHUNK = 128
N_BRANCH = 3
MOBA_BLOCK = 256
MOBA_TOPK = 3
PAGE_SIZE = 128
D_FF = 4 * D_MODEL
EPS = 1e-6
NEG = -1e30
LOG2_E = 1.4426950408889634
IN_COLS = 3 * ATTN_W + 3 * CONV_W + 2 * GMLP_W + N_BRANCH * D_MODEL
COL_Q, COL_K, COL_V, COL_CB, COL_CC, COL_CH, COL_GU, COL_GV, COL_GL = 0, 1, 2, 3, 4, 5, 6, 7, 8

LANES = 128
SUBLANES = 8
V7X_VMEM_LIMIT = 56 * 1024 * 1024

BF16 = jnp.bfloat16
F32 = jnp.float32
NT_DIMS = (((1,), (1,)), ((), ()))


def _rms_scale(x):
    return lax.rsqrt(jnp.mean(x * x, axis=-1, keepdims=True) + EPS)


def _inproj_kernel(x_ref, g_ref, w_ref, qkg_ref, gmat_ref, o_ref, xn_ref):
    j = pl.program_id(1)

    @pl.when(j == 0)
    def _():
        x = x_ref[...]
        xn_ref[...] = (x * _rms_scale(x) * g_ref[...]).astype(BF16)

    acc = jnp.dot(xn_ref[...], w_ref[...], preferred_element_type=F32)

    @pl.when(j <= COL_K)
    def _():
        sq = acc * acc
        hi = sq.astype(BF16)
        lo = (sq - hi.astype(F32)).astype(BF16)
        gmat = gmat_ref[...]
        gw = gmat.shape[0]
        ms = jnp.concatenate(
            [jnp.dot(hi[:, c:c + gw], gmat, preferred_element_type=F32)
             + jnp.dot(lo[:, c:c + gw], gmat, preferred_element_type=F32) for c in range(0, acc.shape[1], gw)],
            axis=1)
        o_ref[...] = acc * lax.rsqrt(ms + EPS) * qkg_ref[...]

    @pl.when(j > COL_K)
    def _():
        o_ref[...] = acc


def _in_proj(x, g, w_bf, qkg, gmat, tm):
    m = x.shape[0]
    tn = 512
    return pl.pallas_call(
        _inproj_kernel,
        out_shape=jax.ShapeDtypeStruct((m, w_bf.shape[1]), F32),
        grid=(m // tm, w_bf.shape[1] // tn),
        in_specs=[
            pl.BlockSpec((tm, D_MODEL), lambda i, j: (i, 0)),
            pl.BlockSpec((1, D_MODEL), lambda i, j: (0, 0)),
            pl.BlockSpec((D_MODEL, tn), lambda i, j: (0, j)),
            pl.BlockSpec((None, 1, tn), lambda i, j: (jnp.minimum(j, COL_K), 0, 0)),
            pl.BlockSpec(gmat.shape, lambda i, j: (0, 0)),
        ],
        out_specs=pl.BlockSpec((tm, tn), lambda i, j: (i, j)),
        scratch_shapes=[pltpu.VMEM((tm, D_MODEL), BF16)],
        compiler_params=pltpu.CompilerParams(
            dimension_semantics=("parallel", "arbitrary"), vmem_limit_bytes=V7X_VMEM_LIMIT),
        name="in_proj",
    )(x, g, w_bf, qkg, gmat)


def _attn_prompt_kernel(q_ref, k_ref, v_ref, o_ref, kto_ref, vto_ref, kb_ref, vt_ref, km_ref, sel_ref, qs_ref, m_ref, acc_ref,
                        s0_ref, s1_ref, *, group):
    seq = k_ref.shape[0]
    nblk = seq // MOBA_BLOCK
    lane = lax.broadcasted_iota(jnp.int32, (1, LANES), 1)
    lo_lanes = lane < HEAD_DIM
    lo_rows = lax.broadcasted_iota(jnp.int32, (LANES, 1), 0) < HEAD_DIM

    def _init():
        kf = k_ref[...]
        kb_ref[...] = kf.astype(BF16)
        km_ref[...] = jnp.sum(kf.reshape(nblk, MOBA_BLOCK, LANES), axis=1) * (1.0 / MOBA_BLOCK)
        for n in range(nblk):
            v_t = v_ref[n * MOBA_BLOCK:(n + 1) * MOBA_BLOCK, :].T
            kto_ref[:, n * MOBA_BLOCK:(n + 1) * MOBA_BLOCK] = kf[n * MOBA_BLOCK:(n + 1) * MOBA_BLOCK, :].T
            vto_ref[:, n * MOBA_BLOCK:(n + 1) * MOBA_BLOCK] = v_t
            vt_ref[0, n] = jnp.where(lo_rows, v_t, 1.0).astype(BF16)
            vt_ref[1, n] = jnp.where(lo_rows, 1.0, v_t).astype(BF16)

    _init()

    def tile_body(t, carry):
        koff = pl.multiple_of(t * MOBA_BLOCK, MOBA_BLOCK)
        q = q_ref[pl.ds(koff, MOBA_BLOCK), :]
        km = km_ref[...]
        blk = lax.broadcasted_iota(jnp.int32, (nblk, 1), 0)
        krow = lax.broadcasted_iota(jnp.int32, (MOBA_BLOCK, MOBA_BLOCK), 0)
        qcol = lax.broadcasted_iota(jnp.int32, (MOBA_BLOCK, MOBA_BLOCK), 1)
        s_refs = (s0_ref, s1_ref)

        def qk_stage(gi, s_ref):
            gi = jnp.minimum(gi, nblk // group - 1)
            for h in range(2):
                qs = qs_ref[h]
                for u in range(group):
                    off = pl.multiple_of((gi * group + u) * MOBA_BLOCK, MOBA_BLOCK)
                    s_ref[h, u] = lax.dot_general(kb_ref[pl.ds(off, MOBA_BLOCK), :], qs, NT_DIMS,
                                                  preferred_element_type=F32)

        def softmax_pv_stage(gi, s_ref):
            for h in range(2):
                m_old = m_ref[h]
                m_new = m_old
                picks = []
                for u in range(group):
                    picked = sel_ref[h, pl.ds(gi * group + u, 1), :] > 0.5
                    blk_max = jnp.max(s_ref[h, u], axis=0, keepdims=True)
                    m_new = jnp.where(picked, jnp.maximum(m_new, blk_max), m_new)
                    picks.append(picked)
                acc = jnp.exp2(m_old - m_new) * acc_ref[h]
                for u in range(group):
                    p = jnp.exp2(s_ref[h, u] - jnp.where(picks[u], m_new, -NEG)).astype(BF16)
                    acc = acc + jnp.dot(vt_ref[h, gi * group + u], p, preferred_element_type=F32)
                m_ref[h] = m_new
                acc_ref[h] = acc

        gates = []
        s_own = []
        for h in range(2):
            qh = jnp.where(lo_lanes if h == 0 else jnp.logical_not(lo_lanes), q, 0.0)
            gates.append(lax.dot_general(km, qh, NT_DIMS, precision=lax.Precision.HIGHEST,
                                         preferred_element_type=F32))
            qs = (qh * (HEAD_DIM ** -0.5 * LOG2_E)).astype(BF16)
            qs_ref[h] = qs
            s_own.append(lax.dot_general(kb_ref[pl.ds(koff, MOBA_BLOCK), :], qs, NT_DIMS,
                                         preferred_element_type=F32))
        qk_stage(0, s_refs[0])

        for h in range(2):
            g = jnp.where(blk < t, gates[h], -jnp.inf)
            rank = jnp.zeros_like(g)
            for mm in range(nblk):
                gm = g[mm:mm + 1, :]
                beats = jnp.logical_or(gm > g, jnp.logical_and(gm == g, blk > mm))
                rank = rank + jnp.where(beats, 1.0, 0.0)
            sel_ref[h] = jnp.where(jnp.logical_and(rank < float(MOBA_TOPK), blk < t), 1.0, 0.0)

            s = jnp.where(krow <= qcol, s_own[h], NEG)
            m0 = jnp.max(s, axis=0, keepdims=True)
            p = jnp.exp2(s - m0).astype(BF16)
            m_ref[h] = m0
            acc_ref[h] = jnp.dot(vt_ref[h, t], p, preferred_element_type=F32)

        ngroups = (t + group - 1) // group

        def body(j, inner):
            g0 = 2 * j
            qk_stage(g0 + 1, s_refs[1])
            softmax_pv_stage(g0, s_refs[0])
            qk_stage(g0 + 2, s_refs[0])
            softmax_pv_stage(g0 + 1, s_refs[1])
            return inner

        lax.fori_loop(0, (ngroups + 1) // 2, body, 0)

        a0 = acc_ref[0]
        a1 = acc_ref[1]
        o_t = jnp.where(lo_rows, a0 * (1.0 / a0[HEAD_DIM:HEAD_DIM + 1, :]), a1 * (1.0 / a1[0:1, :]))
        o_ref[pl.ds(koff, MOBA_BLOCK), :] = o_t.T.astype(o_ref.dtype)
        return carry

    lax.fori_loop(0, nblk, tile_body, 0)


def _attn_prompt(slab, batch, seq, group=2):
    m = slab.shape[0]
    nq = seq // MOBA_BLOCK
    npair = ATTN_W // LANES
    cpb = 512 // LANES
    assert nq % group == 0
    return pl.pallas_call(
        functools.partial(_attn_prompt_kernel, group=group),
        out_shape=(jax.ShapeDtypeStruct((m, ATTN_W), BF16),
                   jax.ShapeDtypeStruct((batch, ATTN_W, seq), F32),
                   jax.ShapeDtypeStruct((batch, ATTN_W, seq), F32)),
        grid=(batch, npair),
        in_specs=[
            pl.BlockSpec((seq, LANES), lambda b, p: (b, COL_Q * cpb + p)),
            pl.BlockSpec((seq, LANES), lambda b, p: (b, COL_K * cpb + p)),
            pl.BlockSpec((seq, LANES), lambda b, p: (b, COL_V * cpb + p)),
        ],
        out_specs=(pl.BlockSpec((seq, LANES), lambda b, p: (b, p)),
                   pl.BlockSpec((None, LANES, seq), lambda b, p: (b, p, 0)),
                   pl.BlockSpec((None, LANES, seq), lambda b, p: (b, p, 0))),
        scratch_shapes=[
            pltpu.VMEM((seq, LANES), BF16),
            pltpu.VMEM((2, nq, LANES, MOBA_BLOCK), BF16),
            pltpu.VMEM((nq, LANES), F32),
            pltpu.VMEM((2, nq, MOBA_BLOCK), F32),
            pltpu.VMEM((2, MOBA_BLOCK, LANES), BF16),
            pltpu.VMEM((2, 1, MOBA_BLOCK), F32),
            pltpu.VMEM((2, LANES, MOBA_BLOCK), F32),
            pltpu.VMEM((2, group, MOBA_BLOCK, MOBA_BLOCK), F32),
            pltpu.VMEM((2, group, MOBA_BLOCK, MOBA_BLOCK), F32),
        ],
        compiler_params=pltpu.CompilerParams(
            dimension_semantics=("parallel", "parallel"), vmem_limit_bytes=V7X_VMEM_LIMIT),
        name="attn_prompt",
    )(slab, slab, slab)


def _attn_sample_qk(q_ref, kp_refs):
    t_new = q_ref.shape[0]
    q = q_ref[...]
    head_of_lane = lax.shift_right_logical(lax.broadcasted_iota(jnp.int32, (N_HEADS, ATTN_W), 1), 6)
    hm = head_of_lane == lax.broadcasted_iota(jnp.int32, (N_HEADS, ATTN_W), 0)
    qf = jnp.concatenate(
        [jnp.where(hm, jnp.broadcast_to(q[t:t + 1, :], (N_HEADS, ATTN_W)), 0.0) for t in range(t_new)], axis=0)
    hm_rows = jnp.concatenate([hm] * t_new, axis=0)
    qs = qf * (HEAD_DIM ** -0.5)
    qb = qs.astype(BF16)
    s_raw = [jnp.dot(qb, r[...].astype(BF16), preferred_element_type=F32) for r in kp_refs]
    return qf, qs, hm_rows, s_raw


def _attn_sample_softmax(qk, kn_ref, vn_ref, kp_refs):
    qf, qs, hm_rows, s_raw = qk
    n_pages = len(kp_refs)
    t_new = kn_ref.shape[0]
    rows = t_new * N_HEADS
    pages_per_blk = MOBA_BLOCK // PAGE_SIZE
    n_past = n_pages // pages_per_blk
    assert n_past <= LANES
    kn = kn_ref[...]
    trow = lax.shift_right_logical(lax.broadcasted_iota(jnp.int32, (rows, 1), 0), 3)
    lane = lax.broadcasted_iota(jnp.int32, (1, LANES), 1)

    kmean = jnp.zeros((ATTN_W, LANES), F32)
    for n in range(n_past):
        ksum = kp_refs[pages_per_blk * n][...]
        for j in range(1, pages_per_blk):
            ksum = ksum + kp_refs[pages_per_blk * n + j][...]
        km = jnp.sum(ksum, axis=1, keepdims=True) * (1.0 / MOBA_BLOCK)
        kmean = jnp.where(lane == n, km, kmean)
    gate = jnp.dot(qf, kmean, precision=lax.Precision.HIGHEST, preferred_element_type=F32)
    g = jnp.where(lane < n_past, gate, -jnp.inf)
    rank = jnp.zeros_like(g)
    for mm in range(n_past):
        gm = jnp.broadcast_to(g[:, mm:mm + 1], g.shape)
        beats = jnp.logical_or(gm > g, jnp.logical_and(gm == g, lane > mm))
        rank = rank + jnp.where(beats, 1.0, 0.0)
    sel = jnp.where(rank < float(min(MOBA_TOPK, n_past)), 1.0, 0.0)

    s_past = []
    for p in range(n_pages):
        n = p // pages_per_blk
        s_past.append(jnp.where(sel[:, n:n + 1] > 0.5, s_raw[p], NEG))
    s_own = []
    for i in range(t_new):
        s = jnp.sum(qs * kn[i:i + 1, :], axis=1, keepdims=True)
        s_own.append(jnp.where(trow >= i, s, NEG))

    m = s_own[0]
    for s in s_own[1:]:
        m = jnp.maximum(m, s)
    for s in s_past:
        m = jnp.maximum(m, jnp.max(s, axis=1, keepdims=True))

    den = jnp.zeros((rows, 1), F32)
    e_past = []
    for p in range(n_pages):
        e = jnp.exp(s_past[p] - m)
        den = den + jnp.sum(e, axis=1, keepdims=True)
        e_past.append(e.astype(BF16))
    e_own = []
    for i in range(t_new):
        e = jnp.exp(s_own[i] - m)
        den = den + e
        e_own.append(e)
    return e_past, e_own, den, hm_rows, vn_ref[...]


def _attn_sample_output(state, vp_refs, o_ref):
    e_past, e_own, den, hm_rows, vn = state
    t_new = len(e_own)
    acc = jnp.zeros(hm_rows.shape, F32)
    for p, e in enumerate(e_past):
        acc = acc + lax.dot_general(e, vp_refs[p][...].astype(BF16), NT_DIMS, preferred_element_type=F32)
    for i, e in enumerate(e_own):
        acc = acc + e * vn[i:i + 1, :]
    o = jnp.where(hm_rows, acc / den, 0.0)
    o_ref[...] = jnp.sum(o.reshape(t_new, N_HEADS, ATTN_W), axis=1)


def _merge_tail(conv, gm, attn_ref, gate_lin, gb_ref, wb_ref, wo_ref, x_ref, y_ref):
    branches = (attn_ref[...].astype(BF16), conv.astype(BF16), gm.astype(BF16))
    merged = None
    for i in range(N_BRANCH):
        gate = jax.nn.sigmoid(gate_lin(i) + gb_ref[i:i + 1, :])
        term = gate * jnp.dot(branches[i], wb_ref[i], preferred_element_type=F32)
        merged = term if merged is None else merged + term
    y_ref[...] = x_ref[...] + jnp.dot(merged.astype(BF16), wo_ref[...], preferred_element_type=F32)


def _conv_taps(inp, m1, m2, cw_ref):
    return m2 * cw_ref[0:1, :] + m1 * cw_ref[1:2, :] + inp * cw_ref[2:3, :]


def _mixer_prompt_kernel(x_ref, attn_ref, g1_ref, w_ref, cw_ref, gn_ref, ws_ref, gbias_ref, gb_ref, wb_ref, wo_ref,
                         y_ref, tail_ref, hist_ref, *, tiles_per_seq):
    i = pl.program_id(0)
    tm = x_ref.shape[0]
    x = x_ref[...]
    xn = (x * _rms_scale(x) * g1_ref[...]).astype(BF16)

    def proj(c0, c1):
        return jnp.dot(xn, w_ref[:, c0 * 512:c1 * 512], preferred_element_type=F32)

    @pl.when(i == 0)
    def _():
        hist_ref[...] = jnp.zeros_like(hist_ref)

    cch = proj(COL_CC - COL_CB, COL_CH - COL_CB + 1)
    inp = cch[:, :CONV_W] * cch[:, CONV_W:]
    prev = jnp.where(i % tiles_per_seq == 0, 0.0, hist_ref[...])
    p0 = prev[6:7, :]
    p1 = prev[7:8, :]
    rowi = lax.broadcasted_iota(jnp.int32, (tm, 1), 0)
    m1 = jnp.where(rowi == 0, p1, pltpu.roll(inp, 1, axis=0))
    m2 = jnp.where(rowi == 0, p0, jnp.where(rowi == 1, p1, pltpu.roll(inp, 2, axis=0)))
    conv = proj(0, 1) * _conv_taps(inp, m1, m2, cw_ref)
    tail_ref[...] = inp[tm - 8:tm, :]
    hist_ref[...] = inp[tm - 8:tm, :]

    guv = proj(COL_GU - COL_CB, COL_GV - COL_CB + 1)
    gu = guv[:, :GMLP_W]
    gv = guv[:, GMLP_W:]
    vn = gv * _rms_scale(gv) * gn_ref[...]
    lane = lax.broadcasted_iota(jnp.int32, (1, LANES), 1)
    lo_half = lane < (GMLP_W // GMLP_GROUPS)
    wrow = lax.broadcasted_iota(jnp.int32, (CHUNK, 2 * CHUNK), 0)
    wcol = lax.broadcasted_iota(jnp.int32, (CHUNK, 2 * CHUNK), 1) & (CHUNK - 1)
    cols = []
    for c in range(GMLP_W // LANES):
        w = jnp.where(wcol <= wrow, ws_ref[c], 0.0).astype(BF16)
        vc = vn[:, c * LANES:(c + 1) * LANES]
        chunks = []
        for k in range(tm // CHUNK):
            vck = vc[k * CHUNK:(k + 1) * CHUNK, :]
            rhs = jnp.concatenate([jnp.where(lo_half, vck, 0.0), jnp.where(lo_half, 0.0, vck)], axis=0)
            mixed = jnp.dot(w, rhs.astype(BF16), preferred_element_type=F32)
            chunks.append(mixed + gbias_ref[:, c * LANES:(c + 1) * LANES])
        cols.append(jnp.concatenate(chunks, axis=0))
    gm = gu * jnp.concatenate(cols, axis=1)

    gl0 = COL_GL - COL_CB
    _merge_tail(conv, gm, attn_ref, lambda b: proj(gl0 + 2 * b, gl0 + 2 * b + 2), gb_ref, wb_ref, wo_ref,
                x_ref, y_ref)


def _merge_sample_kernel(cb_ref, cc_ref, ch_ref, gu_ref, gv_ref, gl0_ref, gl1_ref, gl2_ref, prev1_ref, prev2_ref,
                         attn_ref, x_ref, cw_ref, gn_ref, coef_ref, gbias_ref, gb_ref, wb_ref, wo_ref,
                         y_ref, inp_ref, *, t_new):
    tm = cb_ref.shape[0]
    inp = cc_ref[...] * ch_ref[...]
    tpos = lax.broadcasted_iota(jnp.int32, (tm, 1), 0) & (t_new - 1)
    m1 = jnp.where(tpos == 0, prev1_ref[...], pltpu.roll(inp, 1, axis=0))
    m2 = jnp.where(tpos < 2, prev2_ref[...], pltpu.roll(inp, 2, axis=0))
    conv = cb_ref[...] * _conv_taps(inp, m1, m2, cw_ref)
    inp_ref[...] = inp

    gv = gv_ref[...]
    vn = gv * _rms_scale(gv) * gn_ref[...]
    mixed = coef_ref[0] * vn + gbias_ref[...]
    for d in range(1, t_new):
        mixed = mixed + coef_ref[d] * pltpu.roll(vn, d, axis=0)
    gm = gu_ref[...] * mixed

    gl_refs = (gl0_ref, gl1_ref, gl2_ref)
    _merge_tail(conv, gm, attn_ref, lambda b: gl_refs[b][...], gb_ref, wb_ref, wo_ref, x_ref, y_ref)


def _slab_specs(tm):
    col = lambda c: pl.BlockSpec((tm, 512), lambda i, c=c: (i, c))
    gl = lambda g: pl.BlockSpec((tm, D_MODEL), lambda i, g=g: (i, COL_GL // 2 + g))
    return [col(COL_CB), col(COL_CC), col(COL_CH), col(COL_GU), col(COL_GV), gl(0), gl(1), gl(2)]


def _full(shape):
    return pl.BlockSpec(shape, lambda i: (0,) * len(shape))


def _resident(shape):
    return pl.BlockSpec(shape, lambda i: (0,) * len(shape), pipeline_mode=pl.Buffered(1))


def _mixer_prompt(x, attn, g1, w_rest, cw, gn, ws_cat, gbias, gb, wb, wo, seq, tm):
    m = x.shape[0]
    ntiles = m // tm
    y, tail = pl.pallas_call(
        functools.partial(_mixer_prompt_kernel, tiles_per_seq=seq // tm),
        out_shape=(jax.ShapeDtypeStruct((m, D_MODEL), F32), jax.ShapeDtypeStruct((ntiles, 8, CONV_W), F32)),
        grid=(ntiles,),
        in_specs=[pl.BlockSpec((tm, D_MODEL), lambda i: (i, 0)),
                  pl.BlockSpec((tm, ATTN_W), lambda i: (i, 0)),
                  _full(g1.shape), _resident(w_rest.shape), _full(cw.shape), _full(gn.shape),
                  _full(ws_cat.shape), _full(gbias.shape), _full(gb.shape),
                  _resident(wb.shape), _resident(wo.shape)],
        out_specs=(pl.BlockSpec((tm, D_MODEL), lambda i: (i, 0)),
                   pl.BlockSpec((None, 8, CONV_W), lambda i: (i, 0, 0))),
        scratch_shapes=[pltpu.VMEM((8, CONV_W), F32)],
        compiler_params=pltpu.CompilerParams(
            dimension_semantics=("arbitrary",), vmem_limit_bytes=V7X_VMEM_LIMIT),
        name="mixer_prompt",
    )(x, attn, g1, w_rest, cw, gn, ws_cat, gbias, gb, wb, wo)
    return y, tail


def _merge_sample(slab, prev1, prev2, attn, x, cw, gn, coef, gbias_rows, gb, wb, wo, t_new, tm):
    m = slab.shape[0]
    rows = lambda w: pl.BlockSpec((tm, w), lambda i: (i, 0))
    y, inp = pl.pallas_call(
        functools.partial(_merge_sample_kernel, t_new=t_new),
        out_shape=(jax.ShapeDtypeStruct((m, D_MODEL), F32), jax.ShapeDtypeStruct((m, CONV_W), F32)),
        grid=(m // tm,),
        in_specs=_slab_specs(tm) + [rows(CONV_W), rows(CONV_W), rows(ATTN_W), rows(D_MODEL),
                                    _full(cw.shape), _full(gn.shape),
                                    pl.BlockSpec((t_new, tm, GMLP_W), lambda i: (0, i, 0)), rows(GMLP_W),
                                    _full(gb.shape), _full(wb.shape), _full(wo.shape)],
        out_specs=(rows(D_MODEL), rows(CONV_W)),
        compiler_params=pltpu.CompilerParams(
            dimension_semantics=("parallel",), vmem_limit_bytes=V7X_VMEM_LIMIT),
        name="merge_sample",
    )(*([slab] * 8), prev1, prev2, attn, x, cw, gn, coef, gbias_rows, gb, wb, wo)
    return y, inp


def _mlp_begin(x_ref, g_ref, xn_ref, acc_ref):
    @pl.when(pl.program_id(1) == 0)
    def _():
        x = x_ref[...]
        xn_ref[...] = (x * _rms_scale(x) * g_ref[...]).astype(BF16)
        acc_ref[...] = jnp.zeros_like(acc_ref)


def _mlp_up(wu_ref, xn_ref):
    h = jnp.maximum(jnp.dot(xn_ref[...], wu_ref[...], preferred_element_type=F32), 0.0)
    return (h * h).astype(BF16)


def _mlp_down(h2, wd_ref, acc_ref):
    acc_ref[...] += jnp.dot(h2, wd_ref[...], preferred_element_type=F32)


def _mlp_end(x_ref, o_ref, acc_ref):
    @pl.when(pl.program_id(1) == pl.num_programs(1) - 1)
    def _():
        o_ref[...] = x_ref[...] + acc_ref[...]


def _mlp_kernel(x_ref, g_ref, wu_ref, wd_ref, o_ref, xn_ref, acc_ref):
    _mlp_begin(x_ref, g_ref, xn_ref, acc_ref)
    _mlp_down(_mlp_up(wu_ref, xn_ref), wd_ref, acc_ref)
    _mlp_end(x_ref, o_ref, acc_ref)


def _mlp_attn_kernel(pt_ref, x_ref, g_ref, wu_ref, wd_ref, q_ref, kn_ref, vn_ref, *refs, n_pages):
    del pt_ref
    kp_refs = refs[:n_pages]
    vp_refs = refs[n_pages:2 * n_pages]
    o_ref, ao_ref, xn_ref, acc_ref = refs[2 * n_pages:]
    _mlp_begin(x_ref, g_ref, xn_ref, acc_ref)
    qk = _attn_sample_qk(q_ref, kp_refs)
    h2 = _mlp_up(wu_ref, xn_ref)
    state = _attn_sample_softmax(qk, kn_ref, vn_ref, kp_refs)
    _mlp_down(h2, wd_ref, acc_ref)
    _attn_sample_output(state, vp_refs, ao_ref)
    _mlp_end(x_ref, o_ref, acc_ref)


def _mlp_attn(x, g, wu, wd, layer, pt_flat, q3, k3, v3, cache_kt, cache_vt, n_pages, tm, tf):
    m = x.shape[0]
    nseq, t_new, _ = q3.shape
    nf = D_FF // tf
    assert (m // tm) * nf == nseq

    def page_spec(p):
        return pl.BlockSpec((None, None, ATTN_W, PAGE_SIZE),
                            lambda i, f, pt: (layer, pt[(i * nf + f) * n_pages + p], 0, 0))

    new_spec = pl.BlockSpec((None, t_new, ATTN_W), lambda i, f, pt: (i * nf + f, 0, 0))
    grid_spec = pltpu.PrefetchScalarGridSpec(
        num_scalar_prefetch=1,
        grid=(m // tm, nf),
        in_specs=[pl.BlockSpec((tm, D_MODEL), lambda i, f, pt: (i, 0)),
                  pl.BlockSpec((1, D_MODEL), lambda i, f, pt: (0, 0)),
                  pl.BlockSpec((D_MODEL, tf), lambda i, f, pt: (0, f)),
                  pl.BlockSpec((tf, D_MODEL), lambda i, f, pt: (f, 0)),
                  new_spec, new_spec, new_spec]
        + [page_spec(p) for p in range(n_pages)] + [page_spec(p) for p in range(n_pages)],
        out_specs=(pl.BlockSpec((tm, D_MODEL), lambda i, f, pt: (i, 0)), new_spec),
        scratch_shapes=[pltpu.VMEM((tm, D_MODEL), BF16), pltpu.VMEM((tm, D_MODEL), F32)],
    )
    return pl.pallas_call(
        functools.partial(_mlp_attn_kernel, n_pages=n_pages),
        out_shape=(jax.ShapeDtypeStruct((m, D_MODEL), F32), jax.ShapeDtypeStruct((nseq, t_new, ATTN_W), F32)),
        grid_spec=grid_spec,
        compiler_params=pltpu.CompilerParams(
            dimension_semantics=("arbitrary", "arbitrary"), vmem_limit_bytes=V7X_VMEM_LIMIT),
        name="mlp_attn",
    )(pt_flat, x, g, wu, wd, q3, k3, v3, *([cache_kt] * n_pages), *([cache_vt] * n_pages))


def _mlp(x, g, wu, wd, tm):
    m = x.shape[0]
    tf = 1024
    return pl.pallas_call(
        _mlp_kernel,
        out_shape=jax.ShapeDtypeStruct((m, D_MODEL), F32),
        grid=(m // tm, D_FF // tf),
        in_specs=[
            pl.BlockSpec((tm, D_MODEL), lambda i, f: (i, 0)),
            pl.BlockSpec((1, D_MODEL), lambda i, f: (0, 0)),
            pl.BlockSpec((D_MODEL, tf), lambda i, f: (0, f)),
            pl.BlockSpec((tf, D_MODEL), lambda i, f: (f, 0)),
        ],
        out_specs=pl.BlockSpec((tm, D_MODEL), lambda i, f: (i, 0)),
        scratch_shapes=[pltpu.VMEM((tm, D_MODEL), BF16), pltpu.VMEM((tm, D_MODEL), F32)],
        compiler_params=pltpu.CompilerParams(
            dimension_semantics=("parallel", "arbitrary"), vmem_limit_bytes=V7X_VMEM_LIMIT),
        name="mlp",
    )(x, g, wu, wd)


MXU_TILE = 256


def _head_mean_matrix():
    idx = np.arange(MXU_TILE) // HEAD_DIM
    return jnp.asarray((idx[:, None] == idx[None, :]).astype(np.float32) / HEAD_DIM, dtype=BF16)


def kernel(x_prompt, x_sample, cache_k, cache_v, state_conv, page_table, norm1_g, w_in, q_norm_g, k_norm_g,
           conv_w, gmlp_norm_g, gmlp_ws, gmlp_b, gate_b, w_branch, w_out, norm2_g, w_up, w_down):
    batch, seq, _ = x_prompt.shape
    nseq, t_new, _ = x_sample.shape
    depth = w_in.shape[0]
    n_pages = page_table.shape[1]
    gdim = GMLP_W // GMLP_GROUPS
    assert t_new == 4 and seq % MOBA_BLOCK == 0 and (n_pages * PAGE_SIZE) % MOBA_BLOCK == 0

    hp = x_prompt.reshape(batch * seq, D_MODEL)
    hs = x_sample.reshape(nseq * t_new, D_MODEL)
    ms = hs.shape[0]
    pt_flat = page_table.reshape(-1).astype(jnp.int32)
    n_phys = cache_k.shape[1]
    cache_kt = cache_k.transpose(0, 1, 3, 4, 2).reshape(depth, n_phys, ATTN_W, PAGE_SIZE)
    cache_vt = cache_v.transpose(0, 1, 3, 4, 2).reshape(depth, n_phys, ATTN_W, PAGE_SIZE)
    gmat = _head_mean_matrix()

    kp_l, vp_l, ks_l, vs_l, cp_l, cs_l, gv_l = [], [], [], [], [], [], []
    for l in range(depth):
        w_in_b = w_in[l].astype(BF16)
        wb_b = w_branch[l].astype(BF16)
        wo_b = w_out[l].astype(BF16)
        wu_b = w_up[l].astype(BF16)
        wd_b = w_down[l].astype(BF16)
        g1 = norm1_g[l].reshape(1, D_MODEL)
        g2 = norm2_g[l].reshape(1, D_MODEL)
        qkg = jnp.stack([jnp.tile(q_norm_g[l], N_HEADS), jnp.tile(k_norm_g[l], N_HEADS)]).reshape(2, 1, ATTN_W)
        gn = gmlp_norm_g[l].reshape(1, GMLP_W)
        ws = gmlp_ws[l]
        ws_cat = ws.reshape(GMLP_GROUPS // 2, 2, CHUNK, CHUNK).transpose(0, 2, 1, 3).reshape(
            GMLP_GROUPS // 2, CHUNK, 2 * CHUNK)
        gbias = jnp.repeat(gmlp_b[l].T, gdim, axis=1)

        slab = _in_proj(hp, g1, w_in_b[:, :COL_CB * 512], qkg, gmat, tm=2048)
        attn, k_t, v_t = _attn_prompt(slab, batch, seq)
        tm_p = 512
        hp, tail = _mixer_prompt(hp, attn, g1, w_in_b[:, COL_CB * 512:], conv_w[l], gn, ws_cat, gbias, gate_b[l],
                                 wb_b, wo_b, seq, tm_p)
        kp_l.append(k_t.reshape(batch, N_HEADS, HEAD_DIM, seq).transpose(0, 3, 1, 2))
        vp_l.append(v_t.reshape(batch, N_HEADS, HEAD_DIM, seq).transpose(0, 3, 1, 2))
        cp_l.append(tail.reshape(batch, seq // tm_p, 8, CONV_W)[:, -1, 6:8, :])

        slab_s = _in_proj(hs, g1, w_in_b, qkg, gmat, tm=ms)
        q3 = slab_s[:, COL_Q * 512:(COL_Q + 1) * 512].reshape(nseq, t_new, ATTN_W)
        k3 = slab_s[:, COL_K * 512:(COL_K + 1) * 512].reshape(nseq, t_new, ATTN_W)
        v3 = slab_s[:, COL_V * 512:(COL_V + 1) * 512].reshape(nseq, t_new, ATTN_W)
        tm_mlp = 1024
        chunks = nseq // (hp.shape[0] // tm_mlp)
        hp, attn_s = _mlp_attn(hp, g2, wu_b, wd_b, l, pt_flat, q3, k3, v3, cache_kt, cache_vt, n_pages,
                               tm=tm_mlp, tf=D_FF // chunks)
        attn_s = attn_s.reshape(ms, ATTN_W)
        st = state_conv[l]
        zero = jnp.zeros((nseq, 1, CONV_W), F32)
        prev1 = jnp.concatenate([st[:, 1:2], zero, zero, zero], axis=1).reshape(ms, CONV_W)
        prev2 = jnp.concatenate([st[:, 0:1], st[:, 1:2], zero, zero], axis=1).reshape(ms, CONV_W)
        tt = np.arange(t_new)
        coef_rows = []
        for d in range(t_new):
            wd_sel = ws[:, tt, np.maximum(tt - d, 0)] * jnp.asarray((tt >= d).astype(np.float32))
            coef_rows.append(jnp.repeat(wd_sel.T, gdim, axis=1))
        coef = jnp.tile(jnp.stack(coef_rows), (1, nseq, 1))
        gbias_rows = jnp.tile(gbias[:t_new], (nseq, 1))
        hs, inp_s = _merge_sample(slab_s, prev1, prev2, attn_s, hs, conv_w[l], gn, coef, gbias_rows, gate_b[l],
                                  wb_b, wo_b, t_new, tm=ms)
        hs = _mlp(hs, g2, wu_b, wd_b, tm=ms)
        ks_l.append(k3.reshape(nseq, t_new, N_HEADS, HEAD_DIM))
        vs_l.append(v3.reshape(nseq, t_new, N_HEADS, HEAD_DIM))
        cs_l.append(inp_s.reshape(nseq, t_new, CONV_W)[:, t_new - 2:, :])
        gv_l.append(slab_s[:, COL_GV * 512:(COL_GV + 1) * 512].reshape(nseq, t_new, GMLP_W))

    return (hp.reshape(batch, seq, D_MODEL), hs.reshape(nseq, t_new, D_MODEL),
            jnp.stack(kp_l), jnp.stack(vp_l), jnp.stack(ks_l), jnp.stack(vs_l),
            jnp.stack(cp_l), jnp.stack(cs_l), jnp.stack(gv_l))
```

```python
import functools

import numpy as np
import jax
import jax.numpy as jnp
from jax import lax
from jax.experimental import pallas as pl
from jax.experimental.pallas import tpu as pltpu

D_MODEL = 1024
N_HEADS = 8
HEAD_DIM = 64
ATTN_W = N_HEADS * HEAD_DIM
CONV_W = 512
GMLP_W = 512
GMLP_GROUPS = 8
CHUNK = 128
N_BRANCH = 3
MOBA_BLOCK = 256
MOBA_TOPK = 3
PAGE_SIZE = 128
D_FF = 4 * D_MODEL
EPS = 1e-6
NEG = -1e30
LOG2_E = 1.4426950408889634
IN_COLS = 3 * ATTN_W + 3 * CONV_W + 2 * GMLP_W + N_BRANCH * D_MODEL
COL_Q, COL_K, COL_V, COL_CB, COL_CC, COL_CH, COL_GU, COL_GV, COL_GL = 0, 1, 2, 3, 4, 5, 6, 7, 8

LANES = 128
SUBLANES = 8
V7X_VMEM_LIMIT = 56 * 1024 * 1024

BF16 = jnp.bfloat16
F32 = jnp.float32
NT_DIMS = (((1,), (1,)), ((), ()))


def _rms_scale(x):
    return lax.rsqrt(jnp.mean(x * x, axis=-1, keepdims=True) + EPS)


def _inproj_kernel(x_ref, g_ref, w_ref, qkg_ref, gmat_ref, o_ref, xn_ref):
    j = pl.program_id(1)

    @pl.when(j == 0)
    def _():
        x = x_ref[...]
        xn_ref[...] = (x * _rms_scale(x) * g_ref[...]).astype(BF16)

    acc = jnp.dot(xn_ref[...], w_ref[...], preferred_element_type=F32)

    @pl.when(j <= COL_K)
    def _():
        sq = acc * acc
        hi = sq.astype(BF16)
        lo = (sq - hi.astype(F32)).astype(BF16)
        gmat = gmat_ref[...]
        gw = gmat.shape[0]
        ms = jnp.concatenate(
            [jnp.dot(hi[:, c:c + gw], gmat, preferred_element_type=F32)
             + jnp.dot(lo[:, c:c + gw], gmat, preferred_element_type=F32) for c in range(0, acc.shape[1], gw)],
            axis=1)
        o_ref[...] = acc * lax.rsqrt(ms + EPS) * qkg_ref[...]

    @pl.when(j > COL_K)
    def _():
        o_ref[...] = acc


def _of_layer(arr, layer, **kw):
    return pl.BlockSpec((None,) + tuple(arr.shape[1:]), lambda *_: (layer,) + (0,) * (arr.ndim - 1), **kw)


def _in_proj(x, layer, g, w_bf, qkg, gmat, tm, ncols):
    m = x.shape[0]
    tn = 512
    return pl.pallas_call(
        _inproj_kernel,
        out_shape=jax.ShapeDtypeStruct((m, ncols), F32),
        grid=(m // tm, ncols // tn),
        in_specs=[
            pl.BlockSpec((tm, D_MODEL), lambda i, j: (i, 0)),
            _of_layer(g, layer),
            pl.BlockSpec((None, D_MODEL, tn), lambda i, j: (layer, 0, j)),
            pl.BlockSpec((None, None, 1, tn), lambda i, j: (layer, jnp.minimum(j, COL_K), 0, 0)),
            pl.BlockSpec(gmat.shape, lambda i, j: (0, 0)),
        ],
        out_specs=pl.BlockSpec((tm, tn), lambda i, j: (i, j)),
        scratch_shapes=[pltpu.VMEM((tm, D_MODEL), BF16)],
        compiler_params=pltpu.CompilerParams(
            dimension_semantics=("parallel", "arbitrary"), vmem_limit_bytes=V7X_VMEM_LIMIT),
        name="in_proj",
    )(x, g, w_bf, qkg, gmat)


def _attn_prompt_kernel(q_ref, k_ref, v_ref, o_ref, kto_ref, vto_ref, kb_ref, vt_ref, km_ref, sel_ref, qs_ref, m_ref, acc_ref,
                        s0_ref, s1_ref, *, group):
    seq = k_ref.shape[0]
    nblk = seq // MOBA_BLOCK
    lane = lax.broadcasted_iota(jnp.int32, (1, LANES), 1)
    lo_lanes = lane < HEAD_DIM
    lo_rows = lax.broadcasted_iota(jnp.int32, (LANES, 1), 0) < HEAD_DIM

    def _init():
        kf = k_ref[...]
        kb_ref[...] = kf.astype(BF16)
        km_ref[...] = jnp.sum(kf.reshape(nblk, MOBA_BLOCK, LANES), axis=1) * (1.0 / MOBA_BLOCK)
        for n in range(nblk):
            v_t = v_ref[n * MOBA_BLOCK:(n + 1) * MOBA_BLOCK, :].T
            kto_ref[:, n * MOBA_BLOCK:(n + 1) * MOBA_BLOCK] = kf[n * MOBA_BLOCK:(n + 1) * MOBA_BLOCK, :].T
            vto_ref[:, n * MOBA_BLOCK:(n + 1) * MOBA_BLOCK] = v_t
            vt_ref[0, n] = jnp.where(lo_rows, v_t, 1.0).astype(BF16)
            vt_ref[1, n] = jnp.where(lo_rows, 1.0, v_t).astype(BF16)

    _init()

    def tile_body(t, carry):
        koff = pl.multiple_of(t * MOBA_BLOCK, MOBA_BLOCK)
        q = q_ref[pl.ds(koff, MOBA_BLOCK), :]
        km = km_ref[...]
        blk = lax.broadcasted_iota(jnp.int32, (nblk, 1), 0)
        krow = lax.broadcasted_iota(jnp.int32, (MOBA_BLOCK, MOBA_BLOCK), 0)
        qcol = lax.broadcasted_iota(jnp.int32, (MOBA_BLOCK, MOBA_BLOCK), 1)
        s_refs = (s0_ref, s1_ref)

        def qk_stage(gi, s_ref):
            gi = jnp.minimum(gi, nblk // group - 1)
            for h in range(2):
                qs = qs_ref[h]
                for u in range(group):
                    off = pl.multiple_of((gi * group + u) * MOBA_BLOCK, MOBA_BLOCK)
                    s_ref[h, u] = lax.dot_general(kb_ref[pl.ds(off, MOBA_BLOCK), :], qs, NT_DIMS,
                                                  preferred_element_type=F32)

        def softmax_pv_stage(gi, s_ref):
            for h in range(2):
                m_old = m_ref[h]
                m_new = m_old
                picks = []
                for u in range(group):
                    picked = sel_ref[h, pl.ds(gi * group + u, 1), :] > 0.5
                    blk_max = jnp.max(s_ref[h, u], axis=0, keepdims=True)
                    m_new = jnp.where(picked, jnp.maximum(m_new, blk_max), m_new)
                    picks.append(picked)
                acc = jnp.exp2(m_old - m_new) * acc_ref[h]
                for u in range(group):
                    p = jnp.exp2(s_ref[h, u] - jnp.where(picks[u], m_new, -NEG)).astype(BF16)
                    acc = acc + jnp.dot(vt_ref[h, gi * group + u], p, preferred_element_type=F32)
                m_ref[h] = m_new
                acc_ref[h] = acc

        gates = []
        s_own = []
        for h in range(2):
            qh = jnp.where(lo_lanes if h == 0 else jnp.logical_not(lo_lanes), q, 0.0)
            gates.append(lax.dot_general(km, qh, NT_DIMS, precision=lax.Precision.HIGHEST,
                                         preferred_element_type=F32))
            qs = (qh * (HEAD_DIM ** -0.5 * LOG2_E)).astype(BF16)
            qs_ref[h] = qs
            s_own.append(lax.dot_general(kb_ref[pl.ds(koff, MOBA_BLOCK), :], qs, NT_DIMS,
                                         preferred_element_type=F32))
        qk_stage(0, s_refs[0])

        for h in range(2):
            g = jnp.where(blk < t, gates[h], -jnp.inf)
            rank = jnp.zeros_like(g)
            for mm in range(nblk):
                gm = g[mm:mm + 1, :]
                beats = jnp.logical_or(gm > g, jnp.logical_and(gm == g, blk > mm))
                rank = rank + jnp.where(beats, 1.0, 0.0)
            sel_ref[h] = jnp.where(jnp.logical_and(rank < float(MOBA_TOPK), blk < t), 1.0, 0.0)

            s = jnp.where(krow <= qcol, s_own[h], NEG)
            m0 = jnp.max(s, axis=0, keepdims=True)
            p = jnp.exp2(s - m0).astype(BF16)
            m_ref[h] = m0
            acc_ref[h] = jnp.dot(vt_ref[h, t], p, preferred_element_type=F32)

        ngroups = (t + group - 1) // group

        def body(j, inner):
            g0 = 2 * j
            qk_stage(g0 + 1, s_refs[1])
            softmax_pv_stage(g0, s_refs[0])
            qk_stage(g0 + 2, s_refs[0])
            softmax_pv_stage(g0 + 1, s_refs[1])
            return inner

        lax.fori_loop(0, (ngroups + 1) // 2, body, 0)

        a0 = acc_ref[0]
        a1 = acc_ref[1]
        o_t = jnp.where(lo_rows, a0 * (1.0 / a0[HEAD_DIM:HEAD_DIM + 1, :]), a1 * (1.0 / a1[0:1, :]))
        o_ref[pl.ds(koff, MOBA_BLOCK), :] = o_t.T.astype(o_ref.dtype)
        return carry

    lax.fori_loop(0, nblk, tile_body, 0)


def _attn_prompt(slab, batch, seq, group=2):
    m = slab.shape[0]
    nq = seq // MOBA_BLOCK
    npair = ATTN_W // LANES
    cpb = 512 // LANES
    assert nq % group == 0
    return pl.pallas_call(
        functools.partial(_attn_prompt_kernel, group=group),
        out_shape=(jax.ShapeDtypeStruct((m, ATTN_W), BF16),
                   jax.ShapeDtypeStruct((batch, ATTN_W, seq), F32),
                   jax.ShapeDtypeStruct((batch, ATTN_W, seq), F32)),
        grid=(batch, npair),
        in_specs=[
            pl.BlockSpec((seq, LANES), lambda b, p: (b, COL_Q * cpb + p)),
            pl.BlockSpec((seq, LANES), lambda b, p: (b, COL_K * cpb + p)),
            pl.BlockSpec((seq, LANES), lambda b, p: (b, COL_V * cpb + p)),
        ],
        out_specs=(pl.BlockSpec((seq, LANES), lambda b, p: (b, p)),
                   pl.BlockSpec((None, LANES, seq), lambda b, p: (b, p, 0)),
                   pl.BlockSpec((None, LANES, seq), lambda b, p: (b, p, 0))),
        scratch_shapes=[
            pltpu.VMEM((seq, LANES), BF16),
            pltpu.VMEM((2, nq, LANES, MOBA_BLOCK), BF16),
            pltpu.VMEM((nq, LANES), F32),
            pltpu.VMEM((2, nq, MOBA_BLOCK), F32),
            pltpu.VMEM((2, MOBA_BLOCK, LANES), BF16),
            pltpu.VMEM((2, 1, MOBA_BLOCK), F32),
            pltpu.VMEM((2, LANES, MOBA_BLOCK), F32),
            pltpu.VMEM((2, group, MOBA_BLOCK, MOBA_BLOCK), F32),
            pltpu.VMEM((2, group, MOBA_BLOCK, MOBA_BLOCK), F32),
        ],
        compiler_params=pltpu.CompilerParams(
            dimension_semantics=("parallel", "parallel"), vmem_limit_bytes=V7X_VMEM_LIMIT),
        name="attn_prompt",
    )(slab, slab, slab)


def _attn_sample_qk(q_ref, kp_refs):
    t_new = q_ref.shape[0]
    q = q_ref[...]
    head_of_lane = lax.shift_right_logical(lax.broadcasted_iota(jnp.int32, (N_HEADS, ATTN_W), 1), 6)
    hm = head_of_lane == lax.broadcasted_iota(jnp.int32, (N_HEADS, ATTN_W), 0)
    qf = jnp.concatenate(
        [jnp.where(hm, jnp.broadcast_to(q[t:t + 1, :], (N_HEADS, ATTN_W)), 0.0) for t in range(t_new)], axis=0)
    hm_rows = jnp.concatenate([hm] * t_new, axis=0)
    qs = qf * (HEAD_DIM ** -0.5)
    qb = qs.astype(BF16)
    s_raw = [jnp.dot(qb, r[...].astype(BF16), preferred_element_type=F32) for r in kp_refs]
    return qf, qs, hm_rows, s_raw


def _attn_sample_softmax(qk, kn_ref, vn_ref, kp_refs):
    qf, qs, hm_rows, s_raw = qk
    n_pages = len(kp_refs)
    t_new = kn_ref.shape[0]
    rows = t_new * N_HEADS
    pages_per_blk = MOBA_BLOCK // PAGE_SIZE
    n_past = n_pages // pages_per_blk
    assert n_past <= LANES
    kn = kn_ref[...]
    trow = lax.shift_right_logical(lax.broadcasted_iota(jnp.int32, (rows, 1), 0), 3)
    lane = lax.broadcasted_iota(jnp.int32, (1, LANES), 1)

    kmean = jnp.zeros((ATTN_W, LANES), F32)
    for n in range(n_past):
        ksum = kp_refs[pages_per_blk * n][...]
        for j in range(1, pages_per_blk):
            ksum = ksum + kp_refs[pages_per_blk * n + j][...]
        km = jnp.sum(ksum, axis=1, keepdims=True) * (1.0 / MOBA_BLOCK)
        kmean = jnp.where(lane == n, km, kmean)
    gate = jnp.dot(qf, kmean, precision=lax.Precision.HIGHEST, preferred_element_type=F32)
    g = jnp.where(lane < n_past, gate, -jnp.inf)
    rank = jnp.zeros_like(g)
    for mm in range(n_past):
        gm = jnp.broadcast_to(g[:, mm:mm + 1], g.shape)
        beats = jnp.logical_or(gm > g, jnp.logical_and(gm == g, lane > mm))
        rank = rank + jnp.where(beats, 1.0, 0.0)
    sel = jnp.where(rank < float(min(MOBA_TOPK, n_past)), 1.0, 0.0)

    s_past = []
    for p in range(n_pages):
        n = p // pages_per_blk
        s_past.append(jnp.where(sel[:, n:n + 1] > 0.5, s_raw[p], NEG))
    s_own = []
    for i in range(t_new):
        s = jnp.sum(qs * kn[i:i + 1, :], axis=1, keepdims=True)
        s_own.append(jnp.where(trow >= i, s, NEG))

    m = s_own[0]
    for s in s_own[1:]:
        m = jnp.maximum(m, s)
    for s in s_past:
        m = jnp.maximum(m, jnp.max(s, axis=1, keepdims=True))

    den = jnp.zeros((rows, 1), F32)
    e_past = []
    for p in range(n_pages):
        e = jnp.exp(s_past[p] - m)
        den = den + jnp.sum(e, axis=1, keepdims=True)
        e_past.append(e.astype(BF16))
    e_own = []
    for i in range(t_new):
        e = jnp.exp(s_own[i] - m)
        den = den + e
        e_own.append(e)
    return e_past, e_own, den, hm_rows, vn_ref[...]


def _attn_sample_output(state, vp_refs, o_ref):
    e_past, e_own, den, hm_rows, vn = state
    t_new = len(e_own)
    acc = jnp.zeros(hm_rows.shape, F32)
    for p, e in enumerate(e_past):
        acc = acc + lax.dot_general(e, vp_refs[p][...].astype(BF16), NT_DIMS, preferred_element_type=F32)
    for i, e in enumerate(e_own):
        acc = acc + e * vn[i:i + 1, :]
    o = jnp.where(hm_rows, acc / den, 0.0)
    o_ref[...] = jnp.sum(o.reshape(t_new, N_HEADS, ATTN_W), axis=1)


def _merge_tail(conv, gm, attn_ref, gate_lin, gb_ref, wb_ref, wo_ref, x_ref, y_ref):
    branches = (attn_ref[...].astype(BF16), conv.astype(BF16), gm.astype(BF16))
    merged = None
    for i in range(N_BRANCH):
        gate = jax.nn.sigmoid(gate_lin(i) + gb_ref[i:i + 1, :])
        term = gate * jnp.dot(branches[i], wb_ref[i], preferred_element_type=F32)
        merged = term if merged is None else merged + term
    y_ref[...] = x_ref[...] + jnp.dot(merged.astype(BF16), wo_ref[...], preferred_element_type=F32)


def _conv_taps(inp, m1, m2, cw_ref):
    return m2 * cw_ref[0:1, :] + m1 * cw_ref[1:2, :] + inp * cw_ref[2:3, :]


def _mixer_prompt_kernel(x_ref, attn_ref, g1_ref, w_ref, cw_ref, gn_ref, ws_ref, gbias_ref, gb_ref, wb_ref, wo_ref,
                         y_ref, tail_ref, hist_ref, *, tiles_per_seq):
    i = pl.program_id(0)
    tm = x_ref.shape[0]
    x = x_ref[...]
    xn = (x * _rms_scale(x) * g1_ref[...]).astype(BF16)

    def proj(c0, c1):
        return jnp.dot(xn, w_ref[:, (COL_CB + c0) * 512:(COL_CB + c1) * 512], preferred_element_type=F32)

    @pl.when(i == 0)
    def _():
        hist_ref[...] = jnp.zeros_like(hist_ref)

    cch = proj(COL_CC - COL_CB, COL_CH - COL_CB + 1)
    inp = cch[:, :CONV_W] * cch[:, CONV_W:]
    prev = jnp.where(i % tiles_per_seq == 0, 0.0, hist_ref[...])
    p0 = prev[6:7, :]
    p1 = prev[7:8, :]
    rowi = lax.broadcasted_iota(jnp.int32, (tm, 1), 0)
    m1 = jnp.where(rowi == 0, p1, pltpu.roll(inp, 1, axis=0))
    m2 = jnp.where(rowi == 0, p0, jnp.where(rowi == 1, p1, pltpu.roll(inp, 2, axis=0)))
    conv = proj(0, 1) * _conv_taps(inp, m1, m2, cw_ref)
    tail_ref[...] = inp[tm - 8:tm, :]
    hist_ref[...] = inp[tm - 8:tm, :]

    guv = proj(COL_GU - COL_CB, COL_GV - COL_CB + 1)
    gu = guv[:, :GMLP_W]
    gv = guv[:, GMLP_W:]
    vn = gv * _rms_scale(gv) * gn_ref[...]
    lane = lax.broadcasted_iota(jnp.int32, (1, LANES), 1)
    lo_half = lane < (GMLP_W // GMLP_GROUPS)
    wrow = lax.broadcasted_iota(jnp.int32, (CHUNK, 2 * CHUNK), 0)
    wcol = lax.broadcasted_iota(jnp.int32, (CHUNK, 2 * CHUNK), 1) & (CHUNK - 1)
    cols = []
    for c in range(GMLP_W // LANES):
        w = jnp.where(wcol <= wrow, ws_ref[c], 0.0).astype(BF16)
        vc = vn[:, c * LANES:(c + 1) * LANES]
        chunks = []
        for k in range(tm // CHUNK):
            vck = vc[k * CHUNK:(k + 1) * CHUNK, :]
            rhs = jnp.concatenate([jnp.where(lo_half, vck, 0.0), jnp.where(lo_half, 0.0, vck)], axis=0)
            mixed = jnp.dot(w, rhs.astype(BF16), preferred_element_type=F32)
            chunks.append(mixed + gbias_ref[:, c * LANES:(c + 1) * LANES])
        cols.append(jnp.concatenate(chunks, axis=0))
    gm = gu * jnp.concatenate(cols, axis=1)

    gl0 = COL_GL - COL_CB
    _merge_tail(conv, gm, attn_ref, lambda b: proj(gl0 + 2 * b, gl0 + 2 * b + 2), gb_ref, wb_ref, wo_ref,
                x_ref, y_ref)


def _merge_sample_kernel(cb_ref, cc_ref, ch_ref, gu_ref, gv_ref, gl0_ref, gl1_ref, gl2_ref, prev1_ref, prev2_ref,
                         attn_ref, x_ref, cw_ref, gn_ref, coef_ref, gbias_ref, gb_ref, wb_ref, wo_ref,
                         y_ref, inp_ref, *, t_new):
    tm = cb_ref.shape[0]
    inp = cc_ref[...] * ch_ref[...]
    tpos = lax.broadcasted_iota(jnp.int32, (tm, 1), 0) & (t_new - 1)
    m1 = jnp.where(tpos == 0, prev1_ref[...], pltpu.roll(inp, 1, axis=0))
    m2 = jnp.where(tpos < 2, prev2_ref[...], pltpu.roll(inp, 2, axis=0))
    conv = cb_ref[...] * _conv_taps(inp, m1, m2, cw_ref)
    inp_ref[...] = inp

    gv = gv_ref[...]
    vn = gv * _rms_scale(gv) * gn_ref[...]
    mixed = coef_ref[0] * vn + gbias_ref[...]
    for d in range(1, t_new):
        mixed = mixed + coef_ref[d] * pltpu.roll(vn, d, axis=0)
    gm = gu_ref[...] * mixed

    gl_refs = (gl0_ref, gl1_ref, gl2_ref)
    _merge_tail(conv, gm, attn_ref, lambda b: gl_refs[b][...], gb_ref, wb_ref, wo_ref, x_ref, y_ref)


def _slab_specs(tm):
    col = lambda c: pl.BlockSpec((tm, 512), lambda i, c=c: (i, c))
    gl = lambda g: pl.BlockSpec((tm, D_MODEL), lambda i, g=g: (i, COL_GL // 2 + g))
    return [col(COL_CB), col(COL_CC), col(COL_CH), col(COL_GU), col(COL_GV), gl(0), gl(1), gl(2)]


def _mixer_prompt(x, attn, layer, g1, w_in, cw, gn, ws_cat, gbias, gb, wb, wo, seq, tm):
    m = x.shape[0]
    ntiles = m // tm
    once = dict(pipeline_mode=pl.Buffered(1))
    y, tail = pl.pallas_call(
        functools.partial(_mixer_prompt_kernel, tiles_per_seq=seq // tm),
        out_shape=(jax.ShapeDtypeStruct((m, D_MODEL), F32), jax.ShapeDtypeStruct((ntiles, 8, CONV_W), F32)),
        grid=(ntiles,),
        in_specs=[pl.BlockSpec((tm, D_MODEL), lambda i: (i, 0)),
                  pl.BlockSpec((tm, ATTN_W), lambda i: (i, 0)),
                  _of_layer(g1, layer), _of_layer(w_in, layer, **once), _of_layer(cw, layer), _of_layer(gn, layer),
                  _of_layer(ws_cat, layer), _of_layer(gbias, layer), _of_layer(gb, layer),
                  _of_layer(wb, layer, **once), _of_layer(wo, layer, **once)],
        out_specs=(pl.BlockSpec((tm, D_MODEL), lambda i: (i, 0)),
                   pl.BlockSpec((None, 8, CONV_W), lambda i: (i, 0, 0))),
        scratch_shapes=[pltpu.VMEM((8, CONV_W), F32)],
        compiler_params=pltpu.CompilerParams(
            dimension_semantics=("arbitrary",), vmem_limit_bytes=V7X_VMEM_LIMIT),
        name="mixer_prompt",
    )(x, attn, g1, w_in, cw, gn, ws_cat, gbias, gb, wb, wo)
    return y, tail


def _merge_sample(slab, layer, prev1, prev2, attn, x, cw, gn, coef, gbias_rows, gb, wb, wo, t_new, tm):
    m = slab.shape[0]
    rows = lambda w: pl.BlockSpec((tm, w), lambda i: (i, 0))
    lrows = lambda w: pl.BlockSpec((None, tm, w), lambda i: (layer, i, 0))
    y, inp = pl.pallas_call(
        functools.partial(_merge_sample_kernel, t_new=t_new),
        out_shape=(jax.ShapeDtypeStruct((m, D_MODEL), F32), jax.ShapeDtypeStruct((m, CONV_W), F32)),
        grid=(m // tm,),
        in_specs=_slab_specs(tm) + [lrows(CONV_W), lrows(CONV_W), rows(ATTN_W), rows(D_MODEL),
                                    _of_layer(cw, layer), _of_layer(gn, layer),
                                    pl.BlockSpec((None, t_new, tm, GMLP_W), lambda i: (layer, 0, i, 0)),
                                    lrows(GMLP_W),
                                    _of_layer(gb, layer), _of_layer(wb, layer), _of_layer(wo, layer)],
        out_specs=(rows(D_MODEL), rows(CONV_W)),
        compiler_params=pltpu.CompilerParams(
            dimension_semantics=("parallel",), vmem_limit_bytes=V7X_VMEM_LIMIT),
        name="merge_sample",
    )(*([slab] * 8), prev1, prev2, attn, x, cw, gn, coef, gbias_rows, gb, wb, wo)
    return y, inp


def _mlp_begin(x_ref, g_ref, xn_ref, acc_ref):
    @pl.when(pl.program_id(1) == 0)
    def _():
        x = x_ref[...]
        xn_ref[...] = (x * _rms_scale(x) * g_ref[...]).astype(BF16)
        acc_ref[...] = jnp.zeros_like(acc_ref)


def _mlp_up(wu_ref, xn_ref):
    h = jnp.maximum(jnp.dot(xn_ref[...], wu_ref[...], preferred_element_type=F32), 0.0)
    return (h * h).astype(BF16)


def _mlp_down(h2, wd_ref, acc_ref):
    acc_ref[...] += jnp.dot(h2, wd_ref[...], preferred_element_type=F32)


def _mlp_end(x_ref, o_ref, acc_ref):
    @pl.when(pl.program_id(1) == pl.num_programs(1) - 1)
    def _():
        o_ref[...] = x_ref[...] + acc_ref[...]


def _mlp_kernel(x_ref, g_ref, wu_ref, wd_ref, o_ref, xn_ref, acc_ref):
    _mlp_begin(x_ref, g_ref, xn_ref, acc_ref)
    _mlp_down(_mlp_up(wu_ref, xn_ref), wd_ref, acc_ref)
    _mlp_end(x_ref, o_ref, acc_ref)


def _mlp_attn_kernel(pt_ref, x_ref, g_ref, wu_ref, wd_ref, q_ref, kn_ref, vn_ref, *refs, n_pages):
    del pt_ref
    kp_refs = refs[:n_pages]
    vp_refs = refs[n_pages:2 * n_pages]
    o_ref, ao_ref, xn_ref, acc_ref = refs[2 * n_pages:]
    _mlp_begin(x_ref, g_ref, xn_ref, acc_ref)
    qk = _attn_sample_qk(q_ref, kp_refs)
    h2 = _mlp_up(wu_ref, xn_ref)
    state = _attn_sample_softmax(qk, kn_ref, vn_ref, kp_refs)
    _mlp_down(h2, wd_ref, acc_ref)
    _attn_sample_output(state, vp_refs, ao_ref)
    _mlp_end(x_ref, o_ref, acc_ref)


def _mlp_attn(x, layer, g, wu, wd, pt_flat, q3, k3, v3, cache_kt, cache_vt, n_pages, tm, tf):
    m = x.shape[0]
    nseq, t_new, _ = q3.shape
    nf = D_FF // tf
    assert (m // tm) * nf == nseq

    def page_spec(p):
        return pl.BlockSpec((None, None, ATTN_W, PAGE_SIZE),
                            lambda i, f, pt: (layer, pt[(i * nf + f) * n_pages + p], 0, 0))

    new_spec = pl.BlockSpec((None, t_new, ATTN_W), lambda i, f, pt: (i * nf + f, 0, 0))
    grid_spec = pltpu.PrefetchScalarGridSpec(
        num_scalar_prefetch=1,
        grid=(m // tm, nf),
        in_specs=[pl.BlockSpec((tm, D_MODEL), lambda i, f, pt: (i, 0)),
                  _of_layer(g, layer),
                  pl.BlockSpec((None, D_MODEL, tf), lambda i, f, pt: (layer, 0, f)),
                  pl.BlockSpec((None, tf, D_MODEL), lambda i, f, pt: (layer, f, 0)),
                  new_spec, new_spec, new_spec]
        + [page_spec(p) for p in range(n_pages)] + [page_spec(p) for p in range(n_pages)],
        out_specs=(pl.BlockSpec((tm, D_MODEL), lambda i, f, pt: (i, 0)), new_spec),
        scratch_shapes=[pltpu.VMEM((tm, D_MODEL), BF16), pltpu.VMEM((tm, D_MODEL), F32)],
    )
    return pl.pallas_call(
        functools.partial(_mlp_attn_kernel, n_pages=n_pages),
        out_shape=(jax.ShapeDtypeStruct((m, D_MODEL), F32), jax.ShapeDtypeStruct((nseq, t_new, ATTN_W), F32)),
        grid_spec=grid_spec,
        compiler_params=pltpu.CompilerParams(
            dimension_semantics=("arbitrary", "arbitrary"), vmem_limit_bytes=V7X_VMEM_LIMIT),
        name="mlp_attn",
    )(pt_flat, x, g, wu, wd, q3, k3, v3, *([cache_kt] * n_pages), *([cache_vt] * n_pages))


def _mlp(x, layer, g, wu, wd, tm):
    m = x.shape[0]
    tf = 1024
    return pl.pallas_call(
        _mlp_kernel,
        out_shape=jax.ShapeDtypeStruct((m, D_MODEL), F32),
        grid=(m // tm, D_FF // tf),
        in_specs=[
            pl.BlockSpec((tm, D_MODEL), lambda i, f: (i, 0)),
            _of_layer(g, layer),
            pl.BlockSpec((None, D_MODEL, tf), lambda i, f: (layer, 0, f)),
            pl.BlockSpec((None, tf, D_MODEL), lambda i, f: (layer, f, 0)),
        ],
        out_specs=pl.BlockSpec((tm, D_MODEL), lambda i, f: (i, 0)),
        scratch_shapes=[pltpu.VMEM((tm, D_MODEL), BF16), pltpu.VMEM((tm, D_MODEL), F32)],
        compiler_params=pltpu.CompilerParams(
            dimension_semantics=("parallel", "arbitrary"), vmem_limit_bytes=V7X_VMEM_LIMIT),
        name="mlp",
    )(x, g, wu, wd)


MXU_TILE = 256


def _head_mean_matrix():
    idx = np.arange(MXU_TILE) // HEAD_DIM
    return jnp.asarray((idx[:, None] == idx[None, :]).astype(np.float32) / HEAD_DIM, dtype=BF16)


def kernel(x_prompt, x_sample, cache_k, cache_v, state_conv, page_table, norm1_g, w_in, q_norm_g, k_norm_g,
           conv_w, gmlp_norm_g, gmlp_ws, gmlp_b, gate_b, w_branch, w_out, norm2_g, w_up, w_down):
    batch, seq, _ = x_prompt.shape
    nseq, t_new, _ = x_sample.shape
    depth = w_in.shape[0]
    n_pages = page_table.shape[1]
    gdim = GMLP_W // GMLP_GROUPS
    assert t_new == 4 and seq % MOBA_BLOCK == 0 and (n_pages * PAGE_SIZE) % MOBA_BLOCK == 0

    hp = x_prompt.reshape(batch * seq, D_MODEL)
    hs = x_sample.reshape(nseq * t_new, D_MODEL)
    ms = hs.shape[0]
    pt_flat = page_table.reshape(-1).astype(jnp.int32)
    n_phys = cache_k.shape[1]
    cache_kt = cache_k.transpose(0, 1, 3, 4, 2).reshape(depth, n_phys, ATTN_W, PAGE_SIZE)
    cache_vt = cache_v.transpose(0, 1, 3, 4, 2).reshape(depth, n_phys, ATTN_W, PAGE_SIZE)
    gmat = _head_mean_matrix()

    w_in_b = w_in.astype(BF16)
    wb_b = w_branch.astype(BF16)
    wo_b = w_out.astype(BF16)
    wu_b = w_up.astype(BF16)
    wd_b = w_down.astype(BF16)
    g1 = norm1_g.reshape(depth, 1, D_MODEL)
    g2 = norm2_g.reshape(depth, 1, D_MODEL)
    qkg = jnp.stack([jnp.tile(q_norm_g, (1, N_HEADS)), jnp.tile(k_norm_g, (1, N_HEADS))], axis=1).reshape(
        depth, 2, 1, ATTN_W)
    gn = gmlp_norm_g.reshape(depth, 1, GMLP_W)
    ws_cat = gmlp_ws.reshape(depth, GMLP_GROUPS // 2, 2, CHUNK, CHUNK).transpose(0, 1, 3, 2, 4).reshape(
        depth, GMLP_GROUPS // 2, CHUNK, 2 * CHUNK)
    gbias = jnp.repeat(gmlp_b.transpose(0, 2, 1), gdim, axis=2)
    zero = jnp.zeros((depth, nseq, 1, CONV_W), F32)
    prev1 = jnp.concatenate([state_conv[:, :, 1:2], zero, zero, zero], axis=2).reshape(depth, ms, CONV_W)
    prev2 = jnp.concatenate([state_conv, zero, zero], axis=2).reshape(depth, ms, CONV_W)
    tt = np.arange(t_new)
    ws_head = gmlp_ws[:, :, :t_new, :t_new]
    coef = jnp.stack([ws_head[:, :, tt, np.maximum(tt - d, 0)] * jnp.asarray((tt >= d).astype(np.float32))
                      for d in range(t_new)], axis=1)
    coef = jnp.tile(jnp.repeat(coef.transpose(0, 1, 3, 2), gdim, axis=3), (1, 1, nseq, 1))
    gbias_rows = jnp.tile(gbias[:, :t_new], (1, nseq, 1))

    kp_l, vp_l, ks_l, vs_l, cp_l, cs_l, gv_l = [], [], [], [], [], [], []
    for l in range(depth):
        slab = _in_proj(hp, l, g1, w_in_b, qkg, gmat, tm=2048, ncols=COL_CB * 512)
        attn, k_t, v_t = _attn_prompt(slab, batch, seq)
        tm_p = 512
        hp, tail = _mixer_prompt(hp, attn, l, g1, w_in_b, conv_w, gn, ws_cat, gbias, gate_b, wb_b, wo_b, seq, tm_p)
        kp_l.append(k_t.reshape(batch, N_HEADS, HEAD_DIM, seq).transpose(0, 3, 1, 2))
        vp_l.append(v_t.reshape(batch, N_HEADS, HEAD_DIM, seq).transpose(0, 3, 1, 2))
        cp_l.append(tail.reshape(batch, seq // tm_p, 8, CONV_W)[:, -1, 6:8, :])

        slab_s = _in_proj(hs, l, g1, w_in_b, qkg, gmat, tm=ms, ncols=IN_COLS)
        q3 = slab_s[:, COL_Q * 512:(COL_Q + 1) * 512].reshape(nseq, t_new, ATTN_W)
        k3 = slab_s[:, COL_K * 512:(COL_K + 1) * 512].reshape(nseq, t_new, ATTN_W)
        v3 = slab_s[:, COL_V * 512:(COL_V + 1) * 512].reshape(nseq, t_new, ATTN_W)
        tm_mlp = 1024
        chunks = nseq // (hp.shape[0] // tm_mlp)
        hp, attn_s = _mlp_attn(hp, l, g2, wu_b, wd_b, pt_flat, q3, k3, v3, cache_kt, cache_vt, n_pages,
                               tm=tm_mlp, tf=D_FF // chunks)
        attn_s = attn_s.reshape(ms, ATTN_W)
        hs, inp_s = _merge_sample(slab_s, l, prev1, prev2, attn_s, hs, conv_w, gn, coef, gbias_rows, gate_b,
                                  wb_b, wo_b, t_new, tm=ms)
        hs = _mlp(hs, l, g2, wu_b, wd_b, tm=ms)
        ks_l.append(k3.reshape(nseq, t_new, N_HEADS, HEAD_DIM))
        vs_l.append(v3.reshape(nseq, t_new, N_HEADS, HEAD_DIM))
        cs_l.append(inp_s.reshape(nseq, t_new, CONV_W)[:, t_new - 2:, :])
        gv_l.append(slab_s[:, COL_GV * 512:(COL_GV + 1) * 512].reshape(nseq, t_new, GMLP_W))

    return (hp.reshape(batch, seq, D_MODEL), hs.reshape(nseq, t_new, D_MODEL),
            jnp.stack(kp_l), jnp.stack(vp_l), jnp.stack(ks_l), jnp.stack(vs_l),
            jnp.stack(cp_l), jnp.stack(cs_l), jnp.stack(gv_l))
```

```python
import functools

import numpy as np
import jax
import jax.numpy as jnp
from jax import lax
from jax.experimental import pallas as pl
from jax.experimental.pallas import tpu as pltpu

D_MODEL = 1024
N_HEADS = 8
HEAD_DIM = 64
ATTN_W = N_HEADS * HEAD_DIM
CONV_W = 512
GMLP_W = 512
GMLP_GROUPS = 8
CHUNK = 128
N_BRANCH = 3
MOBA_BLOCK = 256
MOBA_TOPK = 3
PAGE_SIZE = 128
D_FF = 4 * D_MODEL
EPS = 1e-6
NEG = -1e30
LOG2_E = 1.4426950408889634
IN_COLS = 3 * ATTN_W + 3 * CONV_W + 2 * GMLP_W + N_BRANCH * D_MODEL
COL_Q, COL_K, COL_V, COL_CB, COL_CC, COL_CH, COL_GU, COL_GV, COL_GL = 0, 1, 2, 3, 4, 5, 6, 7, 8

LANES = 128
SUBLANES = 8
V7X_VMEM_LIMIT = 56 * 1024 * 1024

BF16 = jnp.bfloat16
F32 = jnp.float32
NT_DIMS = (((1,), (1,)), ((), ()))


def _rms_scale(x):
    return lax.rsqrt(jnp.mean(x * x, axis=-1, keepdims=True) + EPS)


def _inproj_kernel(x_ref, g_ref, w_ref, qkg_ref, gmat_ref, o_ref, xn_ref):
    j = pl.program_id(1)

    @pl.when(j == 0)
    def _():
        x = x_ref[...]
        xn_ref[...] = (x * _rms_scale(x) * g_ref[...]).astype(BF16)

    acc = jnp.dot(xn_ref[...], w_ref[...], preferred_element_type=F32)

    @pl.when(j <= COL_K)
    def _():
        sq = acc * acc
        hi = sq.astype(BF16)
        lo = (sq - hi.astype(F32)).astype(BF16)
        gmat = gmat_ref[...]
        gw = gmat.shape[0]
        ms = jnp.concatenate(
            [jnp.dot(hi[:, c:c + gw], gmat, preferred_element_type=F32)
             + jnp.dot(lo[:, c:c + gw], gmat, preferred_element_type=F32) for c in range(0, acc.shape[1], gw)],
            axis=1)
        o_ref[...] = acc * lax.rsqrt(ms + EPS) * qkg_ref[...]

    @pl.when(j > COL_K)
    def _():
        o_ref[...] = acc


def _of_layer(arr, layer, **kw):
    return pl.BlockSpec((None,) + tuple(arr.shape[1:]), lambda *_: (layer,) + (0,) * (arr.ndim - 1), **kw)


def _in_proj(x, layer, g, w_bf, qkg, gmat, tm, ncols):
    m = x.shape[0]
    tn = 512
    return pl.pallas_call(
        _inproj_kernel,
        out_shape=jax.ShapeDtypeStruct((m, ncols), F32),
        grid=(m // tm, ncols // tn),
        in_specs=[
            pl.BlockSpec((tm, D_MODEL), lambda i, j: (i, 0)),
            _of_layer(g, layer),
            pl.BlockSpec((None, D_MODEL, tn), lambda i, j: (layer, 0, j)),
            pl.BlockSpec((None, None, 1, tn), lambda i, j: (layer, jnp.minimum(j, COL_K), 0, 0)),
            pl.BlockSpec(gmat.shape, lambda i, j: (0, 0)),
        ],
        out_specs=pl.BlockSpec((tm, tn), lambda i, j: (i, j)),
        scratch_shapes=[pltpu.VMEM((tm, D_MODEL), BF16)],
        compiler_params=pltpu.CompilerParams(
            dimension_semantics=("parallel", "arbitrary"), vmem_limit_bytes=V7X_VMEM_LIMIT),
        name="in_proj",
    )(x, g, w_bf, qkg, gmat)


def _attn_prompt_kernel(q_ref, k_ref, v_ref, o_ref, kto_ref, vto_ref, kb_ref, vt_ref, km_ref, sel_ref, qs_ref, m_ref, acc_ref,
                        s0_ref, s1_ref, *, group):
    seq = k_ref.shape[0]
    nblk = seq // MOBA_BLOCK
    lane = lax.broadcasted_iota(jnp.int32, (1, LANES), 1)
    lo_lanes = lane < HEAD_DIM
    lo_rows = lax.broadcasted_iota(jnp.int32, (LANES, 1), 0) < HEAD_DIM

    def _init():
        kf = k_ref[...]
        kb_ref[...] = kf.astype(BF16)
        km_ref[...] = jnp.sum(kf.reshape(nblk, MOBA_BLOCK, LANES), axis=1) * (1.0 / MOBA_BLOCK)
        for n in range(nblk):
            v_t = v_ref[n * MOBA_BLOCK:(n + 1) * MOBA_BLOCK, :].T
            kto_ref[:, n * MOBA_BLOCK:(n + 1) * MOBA_BLOCK] = kf[n * MOBA_BLOCK:(n + 1) * MOBA_BLOCK, :].T
            vto_ref[:, n * MOBA_BLOCK:(n + 1) * MOBA_BLOCK] = v_t
            vt_ref[0, n] = jnp.where(lo_rows, v_t, 1.0).astype(BF16)
            vt_ref[1, n] = jnp.where(lo_rows, 1.0, v_t).astype(BF16)

    _init()

    def tile_body(t, carry):
        koff = pl.multiple_of(t * MOBA_BLOCK, MOBA_BLOCK)
        q = q_ref[pl.ds(koff, MOBA_BLOCK), :]
        km = km_ref[...]
        blk = lax.broadcasted_iota(jnp.int32, (nblk, 1), 0)
        krow = lax.broadcasted_iota(jnp.int32, (MOBA_BLOCK, MOBA_BLOCK), 0)
        qcol = lax.broadcasted_iota(jnp.int32, (MOBA_BLOCK, MOBA_BLOCK), 1)
        s_refs = (s0_ref, s1_ref)

        def qk_stage(gi, s_ref):
            gi = jnp.minimum(gi, nblk // group - 1)
            for h in range(2):
                qs = qs_ref[h]
                for u in range(group):
                    off = pl.multiple_of((gi * group + u) * MOBA_BLOCK, MOBA_BLOCK)
                    s_ref[h, u] = lax.dot_general(kb_ref[pl.ds(off, MOBA_BLOCK), :], qs, NT_DIMS,
                                                  preferred_element_type=F32)

        def softmax_pv_stage(gi, s_ref):
            for h in range(2):
                m_old = m_ref[h]
                m_new = m_old
                picks = []
                for u in range(group):
                    picked = sel_ref[h, pl.ds(gi * group + u, 1), :] > 0.5
                    blk_max = jnp.max(s_ref[h, u], axis=0, keepdims=True)
                    m_new = jnp.where(picked, jnp.maximum(m_new, blk_max), m_new)
                    picks.append(picked)
                acc = jnp.exp2(m_old - m_new) * acc_ref[h]
                for u in range(group):
                    p = jnp.exp2(s_ref[h, u] - jnp.where(picks[u], m_new, -NEG)).astype(BF16)
                    acc = acc + jnp.dot(vt_ref[h, gi * group + u], p, preferred_element_type=F32)
                m_ref[h] = m_new
                acc_ref[h] = acc

        gates = []
        s_own = []
        for h in range(2):
            qh = jnp.where(lo_lanes if h == 0 else jnp.logical_not(lo_lanes), q, 0.0)
            gates.append(lax.dot_general(km, qh, NT_DIMS, precision=lax.Precision.HIGHEST,
                                         preferred_element_type=F32))
            qs = (qh * (HEAD_DIM ** -0.5 * LOG2_E)).astype(BF16)
            qs_ref[h] = qs
            s_own.append(lax.dot_general(kb_ref[pl.ds(koff, MOBA_BLOCK), :], qs, NT_DIMS,
                                         preferred_element_type=F32))
        qk_stage(0, s_refs[0])

        for h in range(2):
            g = jnp.where(blk < t, gates[h], -jnp.inf)
            rank = jnp.zeros_like(g)
            for mm in range(nblk):
                gm = g[mm:mm + 1, :]
                beats = jnp.logical_or(gm > g, jnp.logical_and(gm == g, blk > mm))
                rank = rank + jnp.where(beats, 1.0, 0.0)
            sel_ref[h] = jnp.where(jnp.logical_and(rank < float(MOBA_TOPK), blk < t), 1.0, 0.0)

            s = jnp.where(krow <= qcol, s_own[h], NEG)
            m0 = jnp.max(s, axis=0, keepdims=True)
            p = jnp.exp2(s - m0).astype(BF16)
            m_ref[h] = m0
            acc_ref[h] = jnp.dot(vt_ref[h, t], p, preferred_element_type=F32)

        ngroups = (t + group - 1) // group

        def body(j, inner):
            g0 = 2 * j
            qk_stage(g0 + 1, s_refs[1])
            softmax_pv_stage(g0, s_refs[0])
            qk_stage(g0 + 2, s_refs[0])
            softmax_pv_stage(g0 + 1, s_refs[1])
            return inner

        lax.fori_loop(0, ngroups // 2, body, 0)

        @pl.when(ngroups % 2 == 1)
        def _():
            softmax_pv_stage(ngroups - 1, s_refs[0])

        a0 = acc_ref[0]
        a1 = acc_ref[1]
        o_t = jnp.where(lo_rows, a0 * (1.0 / a0[HEAD_DIM:HEAD_DIM + 1, :]), a1 * (1.0 / a1[0:1, :]))
        o_ref[pl.ds(koff, MOBA_BLOCK), :] = o_t.T.astype(o_ref.dtype)
        return carry

    lax.fori_loop(0, nblk, tile_body, 0)


def _attn_prompt(slab, batch, seq, group=2):
    m = slab.shape[0]
    nq = seq // MOBA_BLOCK
    npair = ATTN_W // LANES
    cpb = 512 // LANES
    assert nq % group == 0
    return pl.pallas_call(
        functools.partial(_attn_prompt_kernel, group=group),
        out_shape=(jax.ShapeDtypeStruct((m, ATTN_W), BF16),
                   jax.ShapeDtypeStruct((batch, ATTN_W, seq), F32),
                   jax.ShapeDtypeStruct((batch, ATTN_W, seq), F32)),
        grid=(batch, npair),
        in_specs=[
            pl.BlockSpec((seq, LANES), lambda b, p: (b, COL_Q * cpb + p)),
            pl.BlockSpec((seq, LANES), lambda b, p: (b, COL_K * cpb + p)),
            pl.BlockSpec((seq, LANES), lambda b, p: (b, COL_V * cpb + p)),
        ],
        out_specs=(pl.BlockSpec((seq, LANES), lambda b, p: (b, p)),
                   pl.BlockSpec((None, LANES, seq), lambda b, p: (b, p, 0)),
                   pl.BlockSpec((None, LANES, seq), lambda b, p: (b, p, 0))),
        scratch_shapes=[
            pltpu.VMEM((seq, LANES), BF16),
            pltpu.VMEM((2, nq, LANES, MOBA_BLOCK), BF16),
            pltpu.VMEM((nq, LANES), F32),
            pltpu.VMEM((2, nq, MOBA_BLOCK), F32),
            pltpu.VMEM((2, MOBA_BLOCK, LANES), BF16),
            pltpu.VMEM((2, 1, MOBA_BLOCK), F32),
            pltpu.VMEM((2, LANES, MOBA_BLOCK), F32),
            pltpu.VMEM((2, group, MOBA_BLOCK, MOBA_BLOCK), F32),
            pltpu.VMEM((2, group, MOBA_BLOCK, MOBA_BLOCK), F32),
        ],
        compiler_params=pltpu.CompilerParams(
            dimension_semantics=("parallel", "parallel"), vmem_limit_bytes=V7X_VMEM_LIMIT),
        name="attn_prompt",
    )(slab, slab, slab)


def _attn_sample_qk(q_ref, kp_refs):
    t_new = q_ref.shape[0]
    q = q_ref[...]
    head_of_lane = lax.shift_right_logical(lax.broadcasted_iota(jnp.int32, (N_HEADS, ATTN_W), 1), 6)
    hm = head_of_lane == lax.broadcasted_iota(jnp.int32, (N_HEADS, ATTN_W), 0)
    qf = jnp.concatenate(
        [jnp.where(hm, jnp.broadcast_to(q[t:t + 1, :], (N_HEADS, ATTN_W)), 0.0) for t in range(t_new)], axis=0)
    hm_rows = jnp.concatenate([hm] * t_new, axis=0)
    qs = qf * (HEAD_DIM ** -0.5)
    qb = qs.astype(BF16)
    s_raw = [jnp.dot(qb, r[...].astype(BF16), preferred_element_type=F32) for r in kp_refs]
    return qf, qs, hm_rows, s_raw


def _attn_sample_softmax(qk, kn_ref, vn_ref, kp_refs):
    qf, qs, hm_rows, s_raw = qk
    n_pages = len(kp_refs)
    t_new = kn_ref.shape[0]
    rows = t_new * N_HEADS
    pages_per_blk = MOBA_BLOCK // PAGE_SIZE
    n_past = n_pages // pages_per_blk
    assert n_past <= LANES
    kn = kn_ref[...]
    trow = lax.shift_right_logical(lax.broadcasted_iota(jnp.int32, (rows, 1), 0), 3)
    lane = lax.broadcasted_iota(jnp.int32, (1, LANES), 1)

    kmean = jnp.zeros((ATTN_W, LANES), F32)
    for n in range(n_past):
        ksum = kp_refs[pages_per_blk * n][...]
        for j in range(1, pages_per_blk):
            ksum = ksum + kp_refs[pages_per_blk * n + j][...]
        km = jnp.sum(ksum, axis=1, keepdims=True) * (1.0 / MOBA_BLOCK)
        kmean = jnp.where(lane == n, km, kmean)
    gate = jnp.dot(qf, kmean, precision=lax.Precision.HIGHEST, preferred_element_type=F32)
    g = jnp.where(lane < n_past, gate, -jnp.inf)
    rank = jnp.zeros_like(g)
    for mm in range(n_past):
        gm = jnp.broadcast_to(g[:, mm:mm + 1], g.shape)
        beats = jnp.logical_or(gm > g, jnp.logical_and(gm == g, lane > mm))
        rank = rank + jnp.where(beats, 1.0, 0.0)
    sel = jnp.where(rank < float(min(MOBA_TOPK, n_past)), 1.0, 0.0)

    s_past = []
    for p in range(n_pages):
        n = p // pages_per_blk
        s_past.append(jnp.where(sel[:, n:n + 1] > 0.5, s_raw[p], NEG))
    s_own = []
    for i in range(t_new):
        s = jnp.sum(qs * kn[i:i + 1, :], axis=1, keepdims=True)
        s_own.append(jnp.where(trow >= i, s, NEG))

    m = s_own[0]
    for s in s_own[1:]:
        m = jnp.maximum(m, s)
    for s in s_past:
        m = jnp.maximum(m, jnp.max(s, axis=1, keepdims=True))

    den = jnp.zeros((rows, 1), F32)
    e_past = []
    for p in range(n_pages):
        e = jnp.exp(s_past[p] - m)
        den = den + jnp.sum(e, axis=1, keepdims=True)
        e_past.append(e.astype(BF16))
    e_own = []
    for i in range(t_new):
        e = jnp.exp(s_own[i] - m)
        den = den + e
        e_own.append(e)
    return e_past, e_own, den, hm_rows, vn_ref[...]


def _attn_sample_output(state, vp_refs, o_ref):
    e_past, e_own, den, hm_rows, vn = state
    t_new = len(e_own)
    acc = jnp.zeros(hm_rows.shape, F32)
    for p, e in enumerate(e_past):
        acc = acc + lax.dot_general(e, vp_refs[p][...].astype(BF16), NT_DIMS, preferred_element_type=F32)
    for i, e in enumerate(e_own):
        acc = acc + e * vn[i:i + 1, :]
    o = jnp.where(hm_rows, acc / den, 0.0)
    o_ref[...] = jnp.sum(o.reshape(t_new, N_HEADS, ATTN_W), axis=1)


def _merge_tail(conv, gm, attn_ref, gate_lin, gb_ref, wb_ref, wo_ref, x_ref, y_ref):
    branches = (attn_ref[...].astype(BF16), conv.astype(BF16), gm.astype(BF16))
    merged = None
    for i in range(N_BRANCH):
        gate = jax.nn.sigmoid(gate_lin(i) + gb_ref[i:i + 1, :])
        term = gate * jnp.dot(branches[i], wb_ref[i], preferred_element_type=F32)
        merged = term if merged is None else merged + term
    y_ref[...] = x_ref[...] + jnp.dot(merged.astype(BF16), wo_ref[...], preferred_element_type=F32)


def _conv_taps(inp, m1, m2, cw_ref):
    return m2 * cw_ref[0:1, :] + m1 * cw_ref[1:2, :] + inp * cw_ref[2:3, :]


def _mixer_prompt_kernel(x_ref, attn_ref, g1_ref, w_ref, cw_ref, gn_ref, ws_ref, gbias_ref, gb_ref, wb_ref, wo_ref,
                         y_ref, tail_ref, hist_ref, *, tiles_per_seq):
    i = pl.program_id(0)
    tm = x_ref.shape[0]
    x = x_ref[...]
    xn = (x * _rms_scale(x) * g1_ref[...]).astype(BF16)

    def proj(c0, c1):
        return jnp.dot(xn, w_ref[:, (COL_CB + c0) * 512:(COL_CB + c1) * 512], preferred_element_type=F32)

    @pl.when(i == 0)
    def _():
        hist_ref[...] = jnp.zeros_like(hist_ref)

    cch = proj(COL_CC - COL_CB, COL_CH - COL_CB + 1)
    inp = cch[:, :CONV_W] * cch[:, CONV_W:]
    prev = jnp.where(i % tiles_per_seq == 0, 0.0, hist_ref[...])
    p0 = prev[6:7, :]
    p1 = prev[7:8, :]
    rowi = lax.broadcasted_iota(jnp.int32, (tm, 1), 0)
    m1 = jnp.where(rowi == 0, p1, pltpu.roll(inp, 1, axis=0))
    m2 = jnp.where(rowi == 0, p0, jnp.where(rowi == 1, p1, pltpu.roll(inp, 2, axis=0)))
    conv = proj(0, 1) * _conv_taps(inp, m1, m2, cw_ref)
    tail_ref[...] = inp[tm - 8:tm, :]
    hist_ref[...] = inp[tm - 8:tm, :]

    guv = proj(COL_GU - COL_CB, COL_GV - COL_CB + 1)
    gu = guv[:, :GMLP_W]
    gv = guv[:, GMLP_W:]
    vn = gv * _rms_scale(gv) * gn_ref[...]
    lane = lax.broadcasted_iota(jnp.int32, (1, LANES), 1)
    lo_half = lane < (GMLP_W // GMLP_GROUPS)
    wrow = lax.broadcasted_iota(jnp.int32, (CHUNK, 2 * CHUNK), 0)
    wcol = lax.broadcasted_iota(jnp.int32, (CHUNK, 2 * CHUNK), 1) & (CHUNK - 1)
    cols = []
    for c in range(GMLP_W // LANES):
        w = jnp.where(wcol <= wrow, ws_ref[c], 0.0).astype(BF16)
        vc = vn[:, c * LANES:(c + 1) * LANES]
        chunks = []
        for k in range(tm // CHUNK):
            vck = vc[k * CHUNK:(k + 1) * CHUNK, :]
            rhs = jnp.concatenate([jnp.where(lo_half, vck, 0.0), jnp.where(lo_half, 0.0, vck)], axis=0)
            mixed = jnp.dot(w, rhs.astype(BF16), preferred_element_type=F32)
            chunks.append(mixed + gbias_ref[:, c * LANES:(c + 1) * LANES])
        cols.append(jnp.concatenate(chunks, axis=0))
    gm = gu * jnp.concatenate(cols, axis=1)

    gl0 = COL_GL - COL_CB
    _merge_tail(conv, gm, attn_ref, lambda b: proj(gl0 + 2 * b, gl0 + 2 * b + 2), gb_ref, wb_ref, wo_ref,
                x_ref, y_ref)


def _merge_sample_kernel(cb_ref, cc_ref, ch_ref, gu_ref, gv_ref, gl0_ref, gl1_ref, gl2_ref, prev1_ref, prev2_ref,
                         attn_ref, x_ref, cw_ref, gn_ref, coef_ref, gbias_ref, gb_ref, wb_ref, wo_ref,
                         y_ref, inp_ref, *, t_new):
    tm = cb_ref.shape[0]
    inp = cc_ref[...] * ch_ref[...]
    tpos = lax.broadcasted_iota(jnp.int32, (tm, 1), 0) & (t_new - 1)
    m1 = jnp.where(tpos == 0, prev1_ref[...], pltpu.roll(inp, 1, axis=0))
    m2 = jnp.where(tpos < 2, prev2_ref[...], pltpu.roll(inp, 2, axis=0))
    conv = cb_ref[...] * _conv_taps(inp, m1, m2, cw_ref)
    inp_ref[...] = inp

    gv = gv_ref[...]
    vn = gv * _rms_scale(gv) * gn_ref[...]
    mixed = coef_ref[0] * vn + gbias_ref[...]
    for d in range(1, t_new):
        mixed = mixed + coef_ref[d] * pltpu.roll(vn, d, axis=0)
    gm = gu_ref[...] * mixed

    gl_refs = (gl0_ref, gl1_ref, gl2_ref)
    _merge_tail(conv, gm, attn_ref, lambda b: gl_refs[b][...], gb_ref, wb_ref, wo_ref, x_ref, y_ref)


def _slab_specs(tm):
    col = lambda c: pl.BlockSpec((tm, 512), lambda i, c=c: (i, c))
    gl = lambda g: pl.BlockSpec((tm, D_MODEL), lambda i, g=g: (i, COL_GL // 2 + g))
    return [col(COL_CB), col(COL_CC), col(COL_CH), col(COL_GU), col(COL_GV), gl(0), gl(1), gl(2)]


def _mixer_prompt(x, attn, layer, g1, w_in, cw, gn, ws_cat, gbias, gb, wb, wo, seq, tm):
    m = x.shape[0]
    ntiles = m // tm
    once = dict(pipeline_mode=pl.Buffered(1))
    y, tail = pl.pallas_call(
        functools.partial(_mixer_prompt_kernel, tiles_per_seq=seq // tm),
        out_shape=(jax.ShapeDtypeStruct((m, D_MODEL), F32), jax.ShapeDtypeStruct((ntiles, 8, CONV_W), F32)),
        grid=(ntiles,),
        in_specs=[pl.BlockSpec((tm, D_MODEL), lambda i: (i, 0)),
                  pl.BlockSpec((tm, ATTN_W), lambda i: (i, 0)),
                  _of_layer(g1, layer), _of_layer(w_in, layer, **once), _of_layer(cw, layer), _of_layer(gn, layer),
                  _of_layer(ws_cat, layer), _of_layer(gbias, layer), _of_layer(gb, layer),
                  _of_layer(wb, layer, **once), _of_layer(wo, layer, **once)],
        out_specs=(pl.BlockSpec((tm, D_MODEL), lambda i: (i, 0)),
                   pl.BlockSpec((None, 8, CONV_W), lambda i: (i, 0, 0))),
        scratch_shapes=[pltpu.VMEM((8, CONV_W), F32)],
        compiler_params=pltpu.CompilerParams(
            dimension_semantics=("arbitrary",), vmem_limit_bytes=V7X_VMEM_LIMIT),
        name="mixer_prompt",
    )(x, attn, g1, w_in, cw, gn, ws_cat, gbias, gb, wb, wo)
    return y, tail


def _merge_sample(slab, layer, prev1, prev2, attn, x, cw, gn, coef, gbias_rows, gb, wb, wo, t_new, tm):
    m = slab.shape[0]
    rows = lambda w: pl.BlockSpec((tm, w), lambda i: (i, 0))
    lrows = lambda w: pl.BlockSpec((None, tm, w), lambda i: (layer, i, 0))
    y, inp = pl.pallas_call(
        functools.partial(_merge_sample_kernel, t_new=t_new),
        out_shape=(jax.ShapeDtypeStruct((m, D_MODEL), F32), jax.ShapeDtypeStruct((m, CONV_W), F32)),
        grid=(m // tm,),
        in_specs=_slab_specs(tm) + [lrows(CONV_W), lrows(CONV_W), rows(ATTN_W), rows(D_MODEL),
                                    _of_layer(cw, layer), _of_layer(gn, layer),
                                    pl.BlockSpec((None, t_new, tm, GMLP_W), lambda i: (layer, 0, i, 0)),
                                    lrows(GMLP_W),
                                    _of_layer(gb, layer), _of_layer(wb, layer), _of_layer(wo, layer)],
        out_specs=(rows(D_MODEL), rows(CONV_W)),
        compiler_params=pltpu.CompilerParams(
            dimension_semantics=("parallel",), vmem_limit_bytes=V7X_VMEM_LIMIT),
        name="merge_sample",
    )(*([slab] * 8), prev1, prev2, attn, x, cw, gn, coef, gbias_rows, gb, wb, wo)
    return y, inp


def _mlp_begin(x_ref, g_ref, xn_ref, acc_ref):
    @pl.when(pl.program_id(1) == 0)
    def _():
        x = x_ref[...]
        xn_ref[...] = (x * _rms_scale(x) * g_ref[...]).astype(BF16)
        acc_ref[...] = jnp.zeros_like(acc_ref)


def _mlp_up(wu_ref, xn_ref):
    h = jnp.maximum(jnp.dot(xn_ref[...], wu_ref[...], preferred_element_type=F32), 0.0)
    return (h * h).astype(BF16)


def _mlp_down(h2, wd_ref, acc_ref):
    acc_ref[...] += jnp.dot(h2, wd_ref[...], preferred_element_type=F32)


def _mlp_end(x_ref, o_ref, acc_ref):
    @pl.when(pl.program_id(1) == pl.num_programs(1) - 1)
    def _():
        o_ref[...] = x_ref[...] + acc_ref[...]


def _mlp_kernel(x_ref, g_ref, wu_ref, wd_ref, o_ref, xn_ref, acc_ref):
    _mlp_begin(x_ref, g_ref, xn_ref, acc_ref)
    _mlp_down(_mlp_up(wu_ref, xn_ref), wd_ref, acc_ref)
    _mlp_end(x_ref, o_ref, acc_ref)


def _mlp_attn_kernel(pt_ref, x_ref, g_ref, wu_ref, wd_ref, q_ref, kn_ref, vn_ref, *refs, n_pages):
    del pt_ref
    kp_refs = refs[:n_pages]
    vp_refs = refs[n_pages:2 * n_pages]
    o_ref, ao_ref, xn_ref, acc_ref = refs[2 * n_pages:]
    _mlp_begin(x_ref, g_ref, xn_ref, acc_ref)
    qk = _attn_sample_qk(q_ref, kp_refs)
    h2 = _mlp_up(wu_ref, xn_ref)
    state = _attn_sample_softmax(qk, kn_ref, vn_ref, kp_refs)
    _mlp_down(h2, wd_ref, acc_ref)
    _attn_sample_output(state, vp_refs, ao_ref)
    _mlp_end(x_ref, o_ref, acc_ref)


def _mlp_attn(x, layer, g, wu, wd, pt_flat, q3, k3, v3, cache_kt, cache_vt, n_pages, tm, tf):
    m = x.shape[0]
    nseq, t_new, _ = q3.shape
    nf = D_FF // tf
    assert (m // tm) * nf == nseq

    def page_spec(p):
        return pl.BlockSpec((None, None, ATTN_W, PAGE_SIZE),
                            lambda i, f, pt: (layer, pt[(i * nf + f) * n_pages + p], 0, 0))

    new_spec = pl.BlockSpec((None, t_new, ATTN_W), lambda i, f, pt: (i * nf + f, 0, 0))
    grid_spec = pltpu.PrefetchScalarGridSpec(
        num_scalar_prefetch=1,
        grid=(m // tm, nf),
        in_specs=[pl.BlockSpec((tm, D_MODEL), lambda i, f, pt: (i, 0)),
                  _of_layer(g, layer),
                  pl.BlockSpec((None, D_MODEL, tf), lambda i, f, pt: (layer, 0, f)),
                  pl.BlockSpec((None, tf, D_MODEL), lambda i, f, pt: (layer, f, 0)),
                  new_spec, new_spec, new_spec]
        + [page_spec(p) for p in range(n_pages)] + [page_spec(p) for p in range(n_pages)],
        out_specs=(pl.BlockSpec((tm, D_MODEL), lambda i, f, pt: (i, 0)), new_spec),
        scratch_shapes=[pltpu.VMEM((tm, D_MODEL), BF16), pltpu.VMEM((tm, D_MODEL), F32)],
    )
    return pl.pallas_call(
        functools.partial(_mlp_attn_kernel, n_pages=n_pages),
        out_shape=(jax.ShapeDtypeStruct((m, D_MODEL), F32), jax.ShapeDtypeStruct((nseq, t_new, ATTN_W), F32)),
        grid_spec=grid_spec,
        compiler_params=pltpu.CompilerParams(
            dimension_semantics=("arbitrary", "arbitrary"), vmem_limit_bytes=V7X_VMEM_LIMIT),
        name="mlp_attn",
    )(pt_flat, x, g, wu, wd, q3, k3, v3, *([cache_kt] * n_pages), *([cache_vt] * n_pages))


def _mlp(x, layer, g, wu, wd, tm):
    m = x.shape[0]
    tf = 1024
    return pl.pallas_call(
        _mlp_kernel,
        out_shape=jax.ShapeDtypeStruct((m, D_MODEL), F32),
        grid=(m // tm, D_FF // tf),
        in_specs=[
            pl.BlockSpec((tm, D_MODEL), lambda i, f: (i, 0)),
            _of_layer(g, layer),
            pl.BlockSpec((None, D_MODEL, tf), lambda i, f: (layer, 0, f)),
            pl.BlockSpec((None, tf, D_MODEL), lambda i, f: (layer, f, 0)),
        ],
        out_specs=pl.BlockSpec((tm, D_MODEL), lambda i, f: (i, 0)),
        scratch_shapes=[pltpu.VMEM((tm, D_MODEL), BF16), pltpu.VMEM((tm, D_MODEL), F32)],
        compiler_params=pltpu.CompilerParams(
            dimension_semantics=("parallel", "arbitrary"), vmem_limit_bytes=V7X_VMEM_LIMIT),
        name="mlp",
    )(x, g, wu, wd)


MXU_TILE = 256


def _head_mean_matrix():
    idx = np.arange(MXU_TILE) // HEAD_DIM
    return jnp.asarray((idx[:, None] == idx[None, :]).astype(np.float32) / HEAD_DIM, dtype=BF16)


def kernel(x_prompt, x_sample, cache_k, cache_v, state_conv, page_table, norm1_g, w_in, q_norm_g, k_norm_g,
           conv_w, gmlp_norm_g, gmlp_ws, gmlp_b, gate_b, w_branch, w_out, norm2_g, w_up, w_down):
    batch, seq, _ = x_prompt.shape
    nseq, t_new, _ = x_sample.shape
    depth = w_in.shape[0]
    n_pages = page_table.shape[1]
    gdim = GMLP_W // GMLP_GROUPS
    assert t_new == 4 and seq % MOBA_BLOCK == 0 and (n_pages * PAGE_SIZE) % MOBA_BLOCK == 0

    hp = x_prompt.reshape(batch * seq, D_MODEL)
    hs = x_sample.reshape(nseq * t_new, D_MODEL)
    ms = hs.shape[0]
    pt_flat = page_table.reshape(-1).astype(jnp.int32)
    n_phys = cache_k.shape[1]
    cache_kt = cache_k.transpose(0, 1, 3, 4, 2).reshape(depth, n_phys, ATTN_W, PAGE_SIZE)
    cache_vt = cache_v.transpose(0, 1, 3, 4, 2).reshape(depth, n_phys, ATTN_W, PAGE_SIZE)
    gmat = _head_mean_matrix()

    w_in_b = w_in.astype(BF16)
    wb_b = w_branch.astype(BF16)
    wo_b = w_out.astype(BF16)
    wu_b = w_up.astype(BF16)
    wd_b = w_down.astype(BF16)
    g1 = norm1_g.reshape(depth, 1, D_MODEL)
    g2 = norm2_g.reshape(depth, 1, D_MODEL)
    qkg = jnp.stack([jnp.tile(q_norm_g, (1, N_HEADS)), jnp.tile(k_norm_g, (1, N_HEADS))], axis=1).reshape(
        depth, 2, 1, ATTN_W)
    gn = gmlp_norm_g.reshape(depth, 1, GMLP_W)
    ws_cat = gmlp_ws.reshape(depth, GMLP_GROUPS // 2, 2, CHUNK, CHUNK).transpose(0, 1, 3, 2, 4).reshape(
        depth, GMLP_GROUPS // 2, CHUNK, 2 * CHUNK)
    gbias = jnp.repeat(gmlp_b.transpose(0, 2, 1), gdim, axis=2)
    zero = jnp.zeros((depth, nseq, 1, CONV_W), F32)
    prev1 = jnp.concatenate([state_conv[:, :, 1:2], zero, zero, zero], axis=2).reshape(depth, ms, CONV_W)
    prev2 = jnp.concatenate([state_conv, zero, zero], axis=2).reshape(depth, ms, CONV_W)
    tt = np.arange(t_new)
    ws_head = gmlp_ws[:, :, :t_new, :t_new]
    coef = jnp.stack([ws_head[:, :, tt, np.maximum(tt - d, 0)] * jnp.asarray((tt >= d).astype(np.float32))
                      for d in range(t_new)], axis=1)
    coef = jnp.tile(jnp.repeat(coef.transpose(0, 1, 3, 2), gdim, axis=3), (1, 1, nseq, 1))
    gbias_rows = jnp.tile(gbias[:, :t_new], (1, nseq, 1))

    kp_l, vp_l, ks_l, vs_l, cp_l, cs_l, gv_l = [], [], [], [], [], [], []
    for l in range(depth):
        slab = _in_proj(hp, l, g1, w_in_b, qkg, gmat, tm=2048, ncols=COL_CB * 512)
        attn, k_t, v_t = _attn_prompt(slab, batch, seq)
        tm_p = 512
        hp, tail = _mixer_prompt(hp, attn, l, g1, w_in_b, conv_w, gn, ws_cat, gbias, gate_b, wb_b, wo_b, seq, tm_p)
        kp_l.append(k_t.reshape(batch, N_HEADS, HEAD_DIM, seq).transpose(0, 3, 1, 2))
        vp_l.append(v_t.reshape(batch, N_HEADS, HEAD_DIM, seq).transpose(0, 3, 1, 2))
        cp_l.append(tail.reshape(batch, seq // tm_p, 8, CONV_W)[:, -1, 6:8, :])

        slab_s = _in_proj(hs, l, g1, w_in_b, qkg, gmat, tm=ms, ncols=IN_COLS)
        q3 = slab_s[:, COL_Q * 512:(COL_Q + 1) * 512].reshape(nseq, t_new, ATTN_W)
        k3 = slab_s[:, COL_K * 512:(COL_K + 1) * 512].reshape(nseq, t_new, ATTN_W)
        v3 = slab_s[:, COL_V * 512:(COL_V + 1) * 512].reshape(nseq, t_new, ATTN_W)
        tm_mlp = 1024
        chunks = nseq // (hp.shape[0] // tm_mlp)
        hp, attn_s = _mlp_attn(hp, l, g2, wu_b, wd_b, pt_flat, q3, k3, v3, cache_kt, cache_vt, n_pages,
                               tm=tm_mlp, tf=D_FF // chunks)
        attn_s = attn_s.reshape(ms, ATTN_W)
        hs, inp_s = _merge_sample(slab_s, l, prev1, prev2, attn_s, hs, conv_w, gn, coef, gbias_rows, gate_b,
                                  wb_b, wo_b, t_new, tm=ms)
        hs = _mlp(hs, l, g2, wu_b, wd_b, tm=ms)
        ks_l.append(k3.reshape(nseq, t_new, N_HEADS, HEAD_DIM))
        vs_l.append(v3.reshape(nseq, t_new, N_HEADS, HEAD_DIM))
        cs_l.append(inp_s.reshape(nseq, t_new, CONV_W)[:, t_new - 2:, :])
        gv_l.append(slab_s[:, COL_GV * 512:(COL_GV + 1) * 512].reshape(nseq, t_new, GMLP_W))

    return (hp.reshape(batch, seq, D_MODEL), hs.reshape(nseq, t_new, D_MODEL),
            jnp.stack(kp_l), jnp.stack(vp_l), jnp.stack(ks_l), jnp.stack(vs_l),
            jnp.stack(cp_l), jnp.stack(cs_l), jnp.stack(gv_l))
```

```python
import functools

import numpy as np
import jax
import jax.numpy as jnp
from jax import lax
from jax.experimental import pallas as pl
from jax.experimental.pallas import tpu as pltpu

D_MODEL = 1024
N_HEADS = 8
HEAD_DIM = 64
ATTN_W = N_HEADS * HEAD_DIM
CONV_W = 512
GMLP_W = 512
GMLP_GROUPS = 8
CHUNK = 128
N_BRANCH = 3
MOBA_BLOCK = 256
MOBA_TOPK = 3
PAGE_SIZE = 128
D_FF = 4 * D_MODEL
EPS = 1e-6
NEG = -1e30
LOG2_E = 1.4426950408889634
IN_COLS = 3 * ATTN_W + 3 * CONV_W + 2 * GMLP_W + N_BRANCH * D_MODEL
COL_Q, COL_K, COL_V, COL_CB, COL_CC, COL_CH, COL_GU, COL_GV, COL_GL = 0, 1, 2, 3, 4, 5, 6, 7, 8

LANES = 128
SUBLANES = 8
V7X_VMEM_LIMIT = 56 * 1024 * 1024

BF16 = jnp.bfloat16
F32 = jnp.float32
NT_DIMS = (((1,), (1,)), ((), ()))


def _rms_scale(x):
    return lax.rsqrt(jnp.mean(x * x, axis=-1, keepdims=True) + EPS)


def _inproj_kernel(x_ref, g_ref, w_ref, qkg_ref, gmat_ref, o_ref, xn_ref):
    j = pl.program_id(1)

    @pl.when(j == 0)
    def _():
        x = x_ref[...]
        xn_ref[...] = (x * _rms_scale(x) * g_ref[...]).astype(BF16)

    acc = jnp.dot(xn_ref[...], w_ref[...], preferred_element_type=F32)

    @pl.when(j <= COL_K)
    def _():
        sq = acc * acc
        hi = sq.astype(BF16)
        lo = (sq - hi.astype(F32)).astype(BF16)
        gmat = gmat_ref[...]
        gw = gmat.shape[0]
        ms = jnp.concatenate(
            [jnp.dot(hi[:, c:c + gw], gmat, preferred_element_type=F32)
             + jnp.dot(lo[:, c:c + gw], gmat, preferred_element_type=F32) for c in range(0, acc.shape[1], gw)],
            axis=1)
        o_ref[...] = acc * lax.rsqrt(ms + EPS) * qkg_ref[...]

    @pl.when(j > COL_K)
    def _():
        o_ref[...] = acc


def _of_layer(arr, layer, **kw):
    return pl.BlockSpec((None,) + tuple(arr.shape[1:]), lambda *_: (layer,) + (0,) * (arr.ndim - 1), **kw)


def _in_proj(x, layer, g, w_bf, qkg, gmat, tm, ncols):
    m = x.shape[0]
    tn = 512
    return pl.pallas_call(
        _inproj_kernel,
        out_shape=jax.ShapeDtypeStruct((m, ncols), F32),
        grid=(m // tm, ncols // tn),
        in_specs=[
            pl.BlockSpec((tm, D_MODEL), lambda i, j: (i, 0)),
            _of_layer(g, layer),
            pl.BlockSpec((None, D_MODEL, tn), lambda i, j: (layer, 0, j)),
            pl.BlockSpec((None, None, 1, tn), lambda i, j: (layer, jnp.minimum(j, COL_K), 0, 0)),
            pl.BlockSpec(gmat.shape, lambda i, j: (0, 0)),
        ],
        out_specs=pl.BlockSpec((tm, tn), lambda i, j: (i, j)),
        scratch_shapes=[pltpu.VMEM((tm, D_MODEL), BF16)],
        compiler_params=pltpu.CompilerParams(
            dimension_semantics=("parallel", "arbitrary"), vmem_limit_bytes=V7X_VMEM_LIMIT),
        name="in_proj",
    )(x, g, w_bf, qkg, gmat)


def _qkv_proj_kernel(x_ref, g_ref, w_ref, qkg_ref, gmat_ref, o_ref):
    x = x_ref[...]
    xn = (x * _rms_scale(x) * g_ref[...]).astype(BF16)
    acc = jnp.dot(xn, w_ref[...], preferred_element_type=F32)
    qk = acc[:, :2 * ATTN_W]
    sq = qk * qk
    hi = sq.astype(BF16)
    lo = (sq - hi.astype(F32)).astype(BF16)
    gmat = gmat_ref[...]
    gw = gmat.shape[0]
    ms = jnp.concatenate(
        [jnp.dot(hi[:, c:c + gw], gmat, preferred_element_type=F32)
         + jnp.dot(lo[:, c:c + gw], gmat, preferred_element_type=F32) for c in range(0, 2 * ATTN_W, gw)], axis=1)
    gain = jnp.concatenate([qkg_ref[0], qkg_ref[1]], axis=1)
    o_ref[:, :2 * ATTN_W] = qk * lax.rsqrt(ms + EPS) * gain
    o_ref[:, 2 * ATTN_W:] = acc[:, 2 * ATTN_W:]


def _qkv_proj(x, layer, g, w_bf, qkg, gmat, tm):
    m = x.shape[0]
    ncols = 3 * ATTN_W
    return pl.pallas_call(
        _qkv_proj_kernel,
        out_shape=jax.ShapeDtypeStruct((m, ncols), F32),
        grid=(m // tm,),
        in_specs=[
            pl.BlockSpec((tm, D_MODEL), lambda i: (i, 0)),
            _of_layer(g, layer),
            pl.BlockSpec((None, D_MODEL, ncols), lambda i: (layer, 0, 0), pipeline_mode=pl.Buffered(1)),
            _of_layer(qkg, layer),
            pl.BlockSpec(gmat.shape, lambda i: (0, 0)),
        ],
        out_specs=pl.BlockSpec((tm, ncols), lambda i: (i, 0)),
        compiler_params=pltpu.CompilerParams(
            dimension_semantics=("parallel",), vmem_limit_bytes=V7X_VMEM_LIMIT),
        name="qkv_proj",
    )(x, g, w_bf, qkg, gmat)


def _attn_prompt_kernel(q_ref, k_ref, v_ref, o_ref, kto_ref, vto_ref, kb_ref, vt_ref, km_ref, sel_ref, qs_ref, m_ref, acc_ref,
                        s0_ref, s1_ref, *, group):
    seq = k_ref.shape[0]
    nblk = seq // MOBA_BLOCK
    lane = lax.broadcasted_iota(jnp.int32, (1, LANES), 1)
    lo_lanes = lane < HEAD_DIM
    lo_rows = lax.broadcasted_iota(jnp.int32, (LANES, 1), 0) < HEAD_DIM

    def _init():
        kf = k_ref[...]
        kb_ref[...] = kf.astype(BF16)
        km_ref[...] = jnp.sum(kf.reshape(nblk, MOBA_BLOCK, LANES), axis=1) * (1.0 / MOBA_BLOCK)
        for n in range(nblk):
            v_t = v_ref[n * MOBA_BLOCK:(n + 1) * MOBA_BLOCK, :].T
            kto_ref[:, n * MOBA_BLOCK:(n + 1) * MOBA_BLOCK] = kf[n * MOBA_BLOCK:(n + 1) * MOBA_BLOCK, :].T
            vto_ref[:, n * MOBA_BLOCK:(n + 1) * MOBA_BLOCK] = v_t
            vt_ref[0, n] = jnp.where(lo_rows, v_t, 1.0).astype(BF16)
            vt_ref[1, n] = jnp.where(lo_rows, 1.0, v_t).astype(BF16)

    _init()

    def tile_body(t, carry):
        koff = pl.multiple_of(t * MOBA_BLOCK, MOBA_BLOCK)
        q = q_ref[pl.ds(koff, MOBA_BLOCK), :]
        km = km_ref[...]
        blk = lax.broadcasted_iota(jnp.int32, (nblk, 1), 0)
        krow = lax.broadcasted_iota(jnp.int32, (MOBA_BLOCK, MOBA_BLOCK), 0)
        qcol = lax.broadcasted_iota(jnp.int32, (MOBA_BLOCK, MOBA_BLOCK), 1)
        s_refs = (s0_ref, s1_ref)

        def qk_stage(gi, s_ref):
            gi = jnp.minimum(gi, nblk // group - 1)
            for h in range(2):
                qs = qs_ref[h]
                for u in range(group):
                    off = pl.multiple_of((gi * group + u) * MOBA_BLOCK, MOBA_BLOCK)
                    s_ref[h, u] = lax.dot_general(kb_ref[pl.ds(off, MOBA_BLOCK), :], qs, NT_DIMS,
                                                  preferred_element_type=F32)

        def softmax_pv_stage(gi, s_ref):
            for h in range(2):
                m_old = m_ref[h]
                m_new = m_old
                picks = []
                for u in range(group):
                    picked = sel_ref[h, pl.ds(gi * group + u, 1), :] > 0.5
                    blk_max = jnp.max(s_ref[h, u], axis=0, keepdims=True)
                    m_new = jnp.where(picked, jnp.maximum(m_new, blk_max), m_new)
                    picks.append(picked)
                acc = jnp.exp2(m_old - m_new) * acc_ref[h]
                for u in range(group):
                    p = jnp.exp2(s_ref[h, u] - jnp.where(picks[u], m_new, -NEG)).astype(BF16)
                    acc = acc + jnp.dot(vt_ref[h, gi * group + u], p, preferred_element_type=F32)
                m_ref[h] = m_new
                acc_ref[h] = acc

        gates = []
        s_own = []
        for h in range(2):
            qh = jnp.where(lo_lanes if h == 0 else jnp.logical_not(lo_lanes), q, 0.0)
            gates.append(lax.dot_general(km, qh, NT_DIMS, precision=lax.Precision.HIGHEST,
                                         preferred_element_type=F32))
            qs = (qh * (HEAD_DIM ** -0.5 * LOG2_E)).astype(BF16)
            qs_ref[h] = qs
            s_own.append(lax.dot_general(kb_ref[pl.ds(koff, MOBA_BLOCK), :], qs, NT_DIMS,
                                         preferred_element_type=F32))
        qk_stage(0, s_refs[0])

        for h in range(2):
            g = jnp.where(blk < t, gates[h], -jnp.inf)
            rank = jnp.zeros_like(g)
            for mm in range(nblk):
                gm = g[mm:mm + 1, :]
                beats = jnp.logical_or(gm > g, jnp.logical_and(gm == g, blk > mm))
                rank = rank + jnp.where(beats, 1.0, 0.0)
            sel_ref[h] = jnp.where(jnp.logical_and(rank < float(MOBA_TOPK), blk < t), 1.0, 0.0)

            s = jnp.where(krow <= qcol, s_own[h], NEG)
            m0 = jnp.max(s, axis=0, keepdims=True)
            p = jnp.exp2(s - m0).astype(BF16)
            m_ref[h] = m0
            acc_ref[h] = jnp.dot(vt_ref[h, t], p, preferred_element_type=F32)

        ngroups = (t + group - 1) // group

        def body(j, inner):
            g0 = 2 * j
            qk_stage(g0 + 1, s_refs[1])
            softmax_pv_stage(g0, s_refs[0])
            qk_stage(g0 + 2, s_refs[0])
            softmax_pv_stage(g0 + 1, s_refs[1])
            return inner

        lax.fori_loop(0, ngroups // 2, body, 0)

        @pl.when(ngroups % 2 == 1)
        def _():
            softmax_pv_stage(ngroups - 1, s_refs[0])

        a0 = acc_ref[0]
        a1 = acc_ref[1]
        o_t = jnp.where(lo_rows, a0 * (1.0 / a0[HEAD_DIM:HEAD_DIM + 1, :]), a1 * (1.0 / a1[0:1, :]))
        o_ref[pl.ds(koff, MOBA_BLOCK), :] = o_t.T.astype(o_ref.dtype)
        return carry

    lax.fori_loop(0, nblk, tile_body, 0)


def _attn_prompt(slab, batch, seq, group=2):
    m = slab.shape[0]
    nq = seq // MOBA_BLOCK
    npair = ATTN_W // LANES
    cpb = 512 // LANES
    assert nq % group == 0
    return pl.pallas_call(
        functools.partial(_attn_prompt_kernel, group=group),
        out_shape=(jax.ShapeDtypeStruct((m, ATTN_W), BF16),
                   jax.ShapeDtypeStruct((batch, ATTN_W, seq), F32),
                   jax.ShapeDtypeStruct((batch, ATTN_W, seq), F32)),
        grid=(batch, npair),
        in_specs=[
            pl.BlockSpec((seq, LANES), lambda b, p: (b, COL_Q * cpb + p)),
            pl.BlockSpec((seq, LANES), lambda b, p: (b, COL_K * cpb + p)),
            pl.BlockSpec((seq, LANES), lambda b, p: (b, COL_V * cpb + p)),
        ],
        out_specs=(pl.BlockSpec((seq, LANES), lambda b, p: (b, p)),
                   pl.BlockSpec((None, LANES, seq), lambda b, p: (b, p, 0)),
                   pl.BlockSpec((None, LANES, seq), lambda b, p: (b, p, 0))),
        scratch_shapes=[
            pltpu.VMEM((seq, LANES), BF16),
            pltpu.VMEM((2, nq, LANES, MOBA_BLOCK), BF16),
            pltpu.VMEM((nq, LANES), F32),
            pltpu.VMEM((2, nq, MOBA_BLOCK), F32),
            pltpu.VMEM((2, MOBA_BLOCK, LANES), BF16),
            pltpu.VMEM((2, 1, MOBA_BLOCK), F32),
            pltpu.VMEM((2, LANES, MOBA_BLOCK), F32),
            pltpu.VMEM((2, group, MOBA_BLOCK, MOBA_BLOCK), F32),
            pltpu.VMEM((2, group, MOBA_BLOCK, MOBA_BLOCK), F32),
        ],
        compiler_params=pltpu.CompilerParams(
            dimension_semantics=("parallel", "parallel"), vmem_limit_bytes=V7X_VMEM_LIMIT),
        name="attn_prompt",
    )(slab, slab, slab)


def _attn_sample_qk(q_ref, kp_refs):
    t_new = q_ref.shape[0]
    q = q_ref[...]
    head_of_lane = lax.shift_right_logical(lax.broadcasted_iota(jnp.int32, (N_HEADS, ATTN_W), 1), 6)
    hm = head_of_lane == lax.broadcasted_iota(jnp.int32, (N_HEADS, ATTN_W), 0)
    qf = jnp.concatenate(
        [jnp.where(hm, jnp.broadcast_to(q[t:t + 1, :], (N_HEADS, ATTN_W)), 0.0) for t in range(t_new)], axis=0)
    hm_rows = jnp.concatenate([hm] * t_new, axis=0)
    qs = qf * (HEAD_DIM ** -0.5)
    qb = qs.astype(BF16)
    s_raw = [jnp.dot(qb, r[...].astype(BF16), preferred_element_type=F32) for r in kp_refs]
    return qf, qs, hm_rows, s_raw


def _attn_sample_softmax(qk, kn_ref, vn_ref, kp_refs):
    qf, qs, hm_rows, s_raw = qk
    n_pages = len(kp_refs)
    t_new = kn_ref.shape[0]
    rows = t_new * N_HEADS
    pages_per_blk = MOBA_BLOCK // PAGE_SIZE
    n_past = n_pages // pages_per_blk
    assert n_past <= LANES
    kn = kn_ref[...]
    trow = lax.shift_right_logical(lax.broadcasted_iota(jnp.int32, (rows, 1), 0), 3)
    lane = lax.broadcasted_iota(jnp.int32, (1, LANES), 1)

    kmean = jnp.zeros((ATTN_W, LANES), F32)
    for n in range(n_past):
        ksum = kp_refs[pages_per_blk * n][...]
        for j in range(1, pages_per_blk):
            ksum = ksum + kp_refs[pages_per_blk * n + j][...]
        km = jnp.sum(ksum, axis=1, keepdims=True) * (1.0 / MOBA_BLOCK)
        kmean = jnp.where(lane == n, km, kmean)
    gate = jnp.dot(qf, kmean, precision=lax.Precision.HIGHEST, preferred_element_type=F32)
    g = jnp.where(lane < n_past, gate, -jnp.inf)
    rank = jnp.zeros_like(g)
    for mm in range(n_past):
        gm = jnp.broadcast_to(g[:, mm:mm + 1], g.shape)
        beats = jnp.logical_or(gm > g, jnp.logical_and(gm == g, lane > mm))
        rank = rank + jnp.where(beats, 1.0, 0.0)
    sel = jnp.where(rank < float(min(MOBA_TOPK, n_past)), 1.0, 0.0)

    s_past = []
    for p in range(n_pages):
        n = p // pages_per_blk
        s_past.append(jnp.where(sel[:, n:n + 1] > 0.5, s_raw[p], NEG))
    s_own = []
    for i in range(t_new):
        s = jnp.sum(qs * kn[i:i + 1, :], axis=1, keepdims=True)
        s_own.append(jnp.where(trow >= i, s, NEG))

    m = s_own[0]
    for s in s_own[1:]:
        m = jnp.maximum(m, s)
    for s in s_past:
        m = jnp.maximum(m, jnp.max(s, axis=1, keepdims=True))

    den = jnp.zeros((rows, 1), F32)
    e_past = []
    for p in range(n_pages):
        e = jnp.exp(s_past[p] - m)
        den = den + jnp.sum(e, axis=1, keepdims=True)
        e_past.append(e.astype(BF16))
    e_own = []
    for i in range(t_new):
        e = jnp.exp(s_own[i] - m)
        den = den + e
        e_own.append(e)
    return e_past, e_own, den, hm_rows, vn_ref[...]


def _attn_sample_output(state, vp_refs, o_ref):
    e_past, e_own, den, hm_rows, vn = state
    t_new = len(e_own)
    acc = jnp.zeros(hm_rows.shape, F32)
    for p, e in enumerate(e_past):
        acc = acc + lax.dot_general(e, vp_refs[p][...].astype(BF16), NT_DIMS, preferred_element_type=F32)
    for i, e in enumerate(e_own):
        acc = acc + e * vn[i:i + 1, :]
    o = jnp.where(hm_rows, acc / den, 0.0)
    o_ref[...] = jnp.sum(o.reshape(t_new, N_HEADS, ATTN_W), axis=1)


def _merge_tail(conv, gm, attn_ref, gate_lin, gb_ref, wb_ref, wo_ref, x_ref, y_ref):
    branches = (attn_ref[...].astype(BF16), conv.astype(BF16), gm.astype(BF16))
    merged = None
    for i in range(N_BRANCH):
        gate = jax.nn.sigmoid(gate_lin(i) + gb_ref[i:i + 1, :])
        term = gate * jnp.dot(branches[i], wb_ref[i], preferred_element_type=F32)
        merged = term if merged is None else merged + term
    y_ref[...] = x_ref[...] + jnp.dot(merged.astype(BF16), wo_ref[...], preferred_element_type=F32)


def _conv_taps(inp, m1, m2, cw_ref):
    return m2 * cw_ref[0:1, :] + m1 * cw_ref[1:2, :] + inp * cw_ref[2:3, :]


def _mixer_prompt_kernel(x_ref, attn_ref, g1_ref, w_ref, cw_ref, gn_ref, ws_ref, gbias_ref, gb_ref, wb_ref, wo_ref,
                         y_ref, tail_ref, hist_ref, *, tiles_per_seq):
    i = pl.program_id(0)
    tm = x_ref.shape[0]
    x = x_ref[...]
    xn = (x * _rms_scale(x) * g1_ref[...]).astype(BF16)

    def proj(c0, c1):
        return jnp.dot(xn, w_ref[:, (COL_CB + c0) * 512:(COL_CB + c1) * 512], preferred_element_type=F32)

    @pl.when(i == 0)
    def _():
        hist_ref[...] = jnp.zeros_like(hist_ref)

    cch = proj(COL_CC - COL_CB, COL_CH - COL_CB + 1)
    inp = cch[:, :CONV_W] * cch[:, CONV_W:]
    prev = jnp.where(i % tiles_per_seq == 0, 0.0, hist_ref[...])
    p0 = prev[6:7, :]
    p1 = prev[7:8, :]
    rowi = lax.broadcasted_iota(jnp.int32, (tm, 1), 0)
    m1 = jnp.where(rowi == 0, p1, pltpu.roll(inp, 1, axis=0))
    m2 = jnp.where(rowi == 0, p0, jnp.where(rowi == 1, p1, pltpu.roll(inp, 2, axis=0)))
    conv = proj(0, 1) * _conv_taps(inp, m1, m2, cw_ref)
    tail_ref[...] = inp[tm - 8:tm, :]
    hist_ref[...] = inp[tm - 8:tm, :]

    guv = proj(COL_GU - COL_CB, COL_GV - COL_CB + 1)
    gu = guv[:, :GMLP_W]
    gv = guv[:, GMLP_W:]
    vn = gv * _rms_scale(gv) * gn_ref[...]
    lane = lax.broadcasted_iota(jnp.int32, (1, LANES), 1)
    lo_half = lane < (GMLP_W // GMLP_GROUPS)
    wrow = lax.broadcasted_iota(jnp.int32, (CHUNK, 2 * CHUNK), 0)
    wcol = lax.broadcasted_iota(jnp.int32, (CHUNK, 2 * CHUNK), 1) & (CHUNK - 1)
    cols = []
    for c in range(GMLP_W // LANES):
        w = jnp.where(wcol <= wrow, ws_ref[c], 0.0).astype(BF16)
        vc = vn[:, c * LANES:(c + 1) * LANES]
        chunks = []
        for k in range(tm // CHUNK):
            vck = vc[k * CHUNK:(k + 1) * CHUNK, :]
            rhs = jnp.concatenate([jnp.where(lo_half, vck, 0.0), jnp.where(lo_half, 0.0, vck)], axis=0)
            mixed = jnp.dot(w, rhs.astype(BF16), preferred_element_type=F32)
            chunks.append(mixed + gbias_ref[:, c * LANES:(c + 1) * LANES])
        cols.append(jnp.concatenate(chunks, axis=0))
    gm = gu * jnp.concatenate(cols, axis=1)

    gl0 = COL_GL - COL_CB
    _merge_tail(conv, gm, attn_ref, lambda b: proj(gl0 + 2 * b, gl0 + 2 * b + 2), gb_ref, wb_ref, wo_ref,
                x_ref, y_ref)


def _merge_sample_kernel(cb_ref, cc_ref, ch_ref, gu_ref, gv_ref, gl0_ref, gl1_ref, gl2_ref, prev1_ref, prev2_ref,
                         attn_ref, x_ref, cw_ref, gn_ref, coef_ref, gbias_ref, gb_ref, wb_ref, wo_ref,
                         y_ref, inp_ref, *, t_new):
    tm = cb_ref.shape[0]
    inp = cc_ref[...] * ch_ref[...]
    tpos = lax.broadcasted_iota(jnp.int32, (tm, 1), 0) & (t_new - 1)
    m1 = jnp.where(tpos == 0, prev1_ref[...], pltpu.roll(inp, 1, axis=0))
    m2 = jnp.where(tpos < 2, prev2_ref[...], pltpu.roll(inp, 2, axis=0))
    conv = cb_ref[...] * _conv_taps(inp, m1, m2, cw_ref)
    inp_ref[...] = inp

    gv = gv_ref[...]
    vn = gv * _rms_scale(gv) * gn_ref[...]
    mixed = coef_ref[0] * vn + gbias_ref[...]
    for d in range(1, t_new):
        mixed = mixed + coef_ref[d] * pltpu.roll(vn, d, axis=0)
    gm = gu_ref[...] * mixed

    gl_refs = (gl0_ref, gl1_ref, gl2_ref)
    _merge_tail(conv, gm, attn_ref, lambda b: gl_refs[b][...], gb_ref, wb_ref, wo_ref, x_ref, y_ref)


def _slab_specs(tm):
    col = lambda c: pl.BlockSpec((tm, 512), lambda i, c=c: (i, c))
    gl = lambda g: pl.BlockSpec((tm, D_MODEL), lambda i, g=g: (i, COL_GL // 2 + g))
    return [col(COL_CB), col(COL_CC), col(COL_CH), col(COL_GU), col(COL_GV), gl(0), gl(1), gl(2)]


def _mixer_prompt(x, attn, layer, g1, w_in, cw, gn, ws_cat, gbias, gb, wb, wo, seq, tm):
    m = x.shape[0]
    ntiles = m // tm
    once = dict(pipeline_mode=pl.Buffered(1))
    y, tail = pl.pallas_call(
        functools.partial(_mixer_prompt_kernel, tiles_per_seq=seq // tm),
        out_shape=(jax.ShapeDtypeStruct((m, D_MODEL), F32), jax.ShapeDtypeStruct((ntiles, 8, CONV_W), F32)),
        grid=(ntiles,),
        in_specs=[pl.BlockSpec((tm, D_MODEL), lambda i: (i, 0)),
                  pl.BlockSpec((tm, ATTN_W), lambda i: (i, 0)),
                  _of_layer(g1, layer), _of_layer(w_in, layer, **once), _of_layer(cw, layer), _of_layer(gn, layer),
                  _of_layer(ws_cat, layer), _of_layer(gbias, layer), _of_layer(gb, layer),
                  _of_layer(wb, layer, **once), _of_layer(wo, layer, **once)],
        out_specs=(pl.BlockSpec((tm, D_MODEL), lambda i: (i, 0)),
                   pl.BlockSpec((None, 8, CONV_W), lambda i: (i, 0, 0))),
        scratch_shapes=[pltpu.VMEM((8, CONV_W), F32)],
        compiler_params=pltpu.CompilerParams(
            dimension_semantics=("arbitrary",), vmem_limit_bytes=V7X_VMEM_LIMIT),
        name="mixer_prompt",
    )(x, attn, g1, w_in, cw, gn, ws_cat, gbias, gb, wb, wo)
    return y, tail


def _merge_sample(slab, layer, prev1, prev2, attn, x, cw, gn, coef, gbias_rows, gb, wb, wo, t_new, tm):
    m = slab.shape[0]
    rows = lambda w: pl.BlockSpec((tm, w), lambda i: (i, 0))
    lrows = lambda w: pl.BlockSpec((None, tm, w), lambda i: (layer, i, 0))
    y, inp = pl.pallas_call(
        functools.partial(_merge_sample_kernel, t_new=t_new),
        out_shape=(jax.ShapeDtypeStruct((m, D_MODEL), F32), jax.ShapeDtypeStruct((m, CONV_W), F32)),
        grid=(m // tm,),
        in_specs=_slab_specs(tm) + [lrows(CONV_W), lrows(CONV_W), rows(ATTN_W), rows(D_MODEL),
                                    _of_layer(cw, layer), _of_layer(gn, layer),
                                    pl.BlockSpec((None, t_new, tm, GMLP_W), lambda i: (layer, 0, i, 0)),
                                    lrows(GMLP_W),
                                    _of_layer(gb, layer), _of_layer(wb, layer), _of_layer(wo, layer)],
        out_specs=(rows(D_MODEL), rows(CONV_W)),
        compiler_params=pltpu.CompilerParams(
            dimension_semantics=("parallel",), vmem_limit_bytes=V7X_VMEM_LIMIT),
        name="merge_sample",
    )(*([slab] * 8), prev1, prev2, attn, x, cw, gn, coef, gbias_rows, gb, wb, wo)
    return y, inp


def _mlp_begin(x_ref, g_ref, xn_ref, acc_ref):
    @pl.when(pl.program_id(1) == 0)
    def _():
        x = x_ref[...]
        xn_ref[...] = (x * _rms_scale(x) * g_ref[...]).astype(BF16)
        acc_ref[...] = jnp.zeros_like(acc_ref)


def _mlp_up(wu_ref, xn_ref):
    h = jnp.maximum(jnp.dot(xn_ref[...], wu_ref[...], preferred_element_type=F32), 0.0)
    return (h * h).astype(BF16)


def _mlp_down(h2, wd_ref, acc_ref):
    acc_ref[...] += jnp.dot(h2, wd_ref[...], preferred_element_type=F32)


def _mlp_end(x_ref, o_ref, acc_ref):
    @pl.when(pl.program_id(1) == pl.num_programs(1) - 1)
    def _():
        o_ref[...] = x_ref[...] + acc_ref[...]


def _mlp_kernel(x_ref, g_ref, wu_ref, wd_ref, o_ref, xn_ref, acc_ref):
    _mlp_begin(x_ref, g_ref, xn_ref, acc_ref)
    _mlp_down(_mlp_up(wu_ref, xn_ref), wd_ref, acc_ref)
    _mlp_end(x_ref, o_ref, acc_ref)


def _mlp_attn_kernel(pt_ref, x_ref, g_ref, wu_ref, wd_ref, q_ref, kn_ref, vn_ref, *refs, n_pages):
    del pt_ref
    kp_refs = refs[:n_pages]
    vp_refs = refs[n_pages:2 * n_pages]
    o_ref, ao_ref, xn_ref, acc_ref = refs[2 * n_pages:]
    _mlp_begin(x_ref, g_ref, xn_ref, acc_ref)
    qk = _attn_sample_qk(q_ref, kp_refs)
    h2 = _mlp_up(wu_ref, xn_ref)
    state = _attn_sample_softmax(qk, kn_ref, vn_ref, kp_refs)
    _mlp_down(h2, wd_ref, acc_ref)
    _attn_sample_output(state, vp_refs, ao_ref)
    _mlp_end(x_ref, o_ref, acc_ref)


def _mlp_attn(x, layer, g, wu, wd, pt_flat, q3, k3, v3, cache_kt, cache_vt, n_pages, tm, tf):
    m = x.shape[0]
    nseq, t_new, _ = q3.shape
    nf = D_FF // tf
    assert (m // tm) * nf == nseq

    def page_spec(p):
        return pl.BlockSpec((None, None, ATTN_W, PAGE_SIZE),
                            lambda i, f, pt: (layer, pt[(i * nf + f) * n_pages + p], 0, 0))

    new_spec = pl.BlockSpec((None, t_new, ATTN_W), lambda i, f, pt: (i * nf + f, 0, 0))
    grid_spec = pltpu.PrefetchScalarGridSpec(
        num_scalar_prefetch=1,
        grid=(m // tm, nf),
        in_specs=[pl.BlockSpec((tm, D_MODEL), lambda i, f, pt: (i, 0)),
                  _of_layer(g, layer),
                  pl.BlockSpec((None, D_MODEL, tf), lambda i, f, pt: (layer, 0, f)),
                  pl.BlockSpec((None, tf, D_MODEL), lambda i, f, pt: (layer, f, 0)),
                  new_spec, new_spec, new_spec]
        + [page_spec(p) for p in range(n_pages)] + [page_spec(p) for p in range(n_pages)],
        out_specs=(pl.BlockSpec((tm, D_MODEL), lambda i, f, pt: (i, 0)), new_spec),
        scratch_shapes=[pltpu.VMEM((tm, D_MODEL), BF16), pltpu.VMEM((tm, D_MODEL), F32)],
    )
    return pl.pallas_call(
        functools.partial(_mlp_attn_kernel, n_pages=n_pages),
        out_shape=(jax.ShapeDtypeStruct((m, D_MODEL), F32), jax.ShapeDtypeStruct((nseq, t_new, ATTN_W), F32)),
        grid_spec=grid_spec,
        compiler_params=pltpu.CompilerParams(
            dimension_semantics=("arbitrary", "arbitrary"), vmem_limit_bytes=V7X_VMEM_LIMIT),
        name="mlp_attn",
    )(pt_flat, x, g, wu, wd, q3, k3, v3, *([cache_kt] * n_pages), *([cache_vt] * n_pages))


def _mlp(x, layer, g, wu, wd, tm):
    m = x.shape[0]
    tf = 1024
    return pl.pallas_call(
        _mlp_kernel,
        out_shape=jax.ShapeDtypeStruct((m, D_MODEL), F32),
        grid=(m // tm, D_FF // tf),
        in_specs=[
            pl.BlockSpec((tm, D_MODEL), lambda i, f: (i, 0)),
            _of_layer(g, layer),
            pl.BlockSpec((None, D_MODEL, tf), lambda i, f: (layer, 0, f)),
            pl.BlockSpec((None, tf, D_MODEL), lambda i, f: (layer, f, 0)),
        ],
        out_specs=pl.BlockSpec((tm, D_MODEL), lambda i, f: (i, 0)),
        scratch_shapes=[pltpu.VMEM((tm, D_MODEL), BF16), pltpu.VMEM((tm, D_MODEL), F32)],
        compiler_params=pltpu.CompilerParams(
            dimension_semantics=("parallel", "arbitrary"), vmem_limit_bytes=V7X_VMEM_LIMIT),
        name="mlp",
    )(x, g, wu, wd)


MXU_TILE = 256


def _head_mean_matrix():
    idx = np.arange(MXU_TILE) // HEAD_DIM
    return jnp.asarray((idx[:, None] == idx[None, :]).astype(np.float32) / HEAD_DIM, dtype=BF16)


def kernel(x_prompt, x_sample, cache_k, cache_v, state_conv, page_table, norm1_g, w_in, q_norm_g, k_norm_g,
           conv_w, gmlp_norm_g, gmlp_ws, gmlp_b, gate_b, w_branch, w_out, norm2_g, w_up, w_down):
    batch, seq, _ = x_prompt.shape
    nseq, t_new, _ = x_sample.shape
    depth = w_in.shape[0]
    n_pages = page_table.shape[1]
    gdim = GMLP_W // GMLP_GROUPS
    assert t_new == 4 and seq % MOBA_BLOCK == 0 and (n_pages * PAGE_SIZE) % MOBA_BLOCK == 0

    hp = x_prompt.reshape(batch * seq, D_MODEL)
    hs = x_sample.reshape(nseq * t_new, D_MODEL)
    ms = hs.shape[0]
    pt_flat = page_table.reshape(-1).astype(jnp.int32)
    n_phys = cache_k.shape[1]
    cache_kt = cache_k.transpose(0, 1, 3, 4, 2).reshape(depth, n_phys, ATTN_W, PAGE_SIZE)
    cache_vt = cache_v.transpose(0, 1, 3, 4, 2).reshape(depth, n_phys, ATTN_W, PAGE_SIZE)
    gmat = _head_mean_matrix()

    w_in_b = w_in.astype(BF16)
    wb_b = w_branch.astype(BF16)
    wo_b = w_out.astype(BF16)
    wu_b = w_up.astype(BF16)
    wd_b = w_down.astype(BF16)
    g1 = norm1_g.reshape(depth, 1, D_MODEL)
    g2 = norm2_g.reshape(depth, 1, D_MODEL)
    qkg = jnp.stack([jnp.tile(q_norm_g, (1, N_HEADS)), jnp.tile(k_norm_g, (1, N_HEADS))], axis=1).reshape(
        depth, 2, 1, ATTN_W)
    gn = gmlp_norm_g.reshape(depth, 1, GMLP_W)
    ws_cat = gmlp_ws.reshape(depth, GMLP_GROUPS // 2, 2, CHUNK, CHUNK).transpose(0, 1, 3, 2, 4).reshape(
        depth, GMLP_GROUPS // 2, CHUNK, 2 * CHUNK)
    gbias = jnp.repeat(gmlp_b.transpose(0, 2, 1), gdim, axis=2)
    zero = jnp.zeros((depth, nseq, 1, CONV_W), F32)
    prev1 = jnp.concatenate([state_conv[:, :, 1:2], zero, zero, zero], axis=2).reshape(depth, ms, CONV_W)
    prev2 = jnp.concatenate([state_conv, zero, zero], axis=2).reshape(depth, ms, CONV_W)
    tt = np.arange(t_new)
    ws_head = gmlp_ws[:, :, :t_new, :t_new]
    coef = jnp.stack([ws_head[:, :, tt, np.maximum(tt - d, 0)] * jnp.asarray((tt >= d).astype(np.float32))
                      for d in range(t_new)], axis=1)
    coef = jnp.tile(jnp.repeat(coef.transpose(0, 1, 3, 2), gdim, axis=3), (1, 1, nseq, 1))
    gbias_rows = jnp.tile(gbias[:, :t_new], (1, nseq, 1))

    kp_l, vp_l, ks_l, vs_l, cp_l, cs_l, gv_l = [], [], [], [], [], [], []
    for l in range(depth):
        slab = _qkv_proj(hp, l, g1, w_in_b, qkg, gmat, tm=1024)
        attn, k_t, v_t = _attn_prompt(slab, batch, seq)
        tm_p = 512
        hp, tail = _mixer_prompt(hp, attn, l, g1, w_in_b, conv_w, gn, ws_cat, gbias, gate_b, wb_b, wo_b, seq, tm_p)
        kp_l.append(k_t.reshape(batch, N_HEADS, HEAD_DIM, seq).transpose(0, 3, 1, 2))
        vp_l.append(v_t.reshape(batch, N_HEADS, HEAD_DIM, seq).transpose(0, 3, 1, 2))
        cp_l.append(tail.reshape(batch, seq // tm_p, 8, CONV_W)[:, -1, 6:8, :])

        slab_s = _in_proj(hs, l, g1, w_in_b, qkg, gmat, tm=ms, ncols=IN_COLS)
        q3 = slab_s[:, COL_Q * 512:(COL_Q + 1) * 512].reshape(nseq, t_new, ATTN_W)
        k3 = slab_s[:, COL_K * 512:(COL_K + 1) * 512].reshape(nseq, t_new, ATTN_W)
        v3 = slab_s[:, COL_V * 512:(COL_V + 1) * 512].reshape(nseq, t_new, ATTN_W)
        tm_mlp = 1024
        chunks = nseq // (hp.shape[0] // tm_mlp)
        hp, attn_s = _mlp_attn(hp, l, g2, wu_b, wd_b, pt_flat, q3, k3, v3, cache_kt, cache_vt, n_pages,
                               tm=tm_mlp, tf=D_FF // chunks)
        attn_s = attn_s.reshape(ms, ATTN_W)
        hs, inp_s = _merge_sample(slab_s, l, prev1, prev2, attn_s, hs, conv_w, gn, coef, gbias_rows, gate_b,
                                  wb_b, wo_b, t_new, tm=ms)
        hs = _mlp(hs, l, g2, wu_b, wd_b, tm=ms)
        ks_l.append(k3.reshape(nseq, t_new, N_HEADS, HEAD_DIM))
        vs_l.append(v3.reshape(nseq, t_new, N_HEADS, HEAD_DIM))
        cs_l.append(inp_s.reshape(nseq, t_new, CONV_W)[:, t_new - 2:, :])
        gv_l.append(slab_s[:, COL_GV * 512:(COL_GV + 1) * 512].reshape(nseq, t_new, GMLP_W))

    return (hp.reshape(batch, seq, D_MODEL), hs.reshape(nseq, t_new, D_MODEL),
            jnp.stack(kp_l), jnp.stack(vp_l), jnp.stack(ks_l), jnp.stack(vs_l),
            jnp.stack(cp_l), jnp.stack(cs_l), jnp.stack(gv_l))
```
